```python
import math
import jax
import jax.numpy as jnp
from jax import lax
import numpy as np

D_MODEL = 1024
BATCH = 4
SEQ = 4096
DEPTH = 2
DEC_BATCH = 32
DEC_SEQ = 8
PAST_LEN = 8192
PAGE_SIZE = 128

N_AB = (DEPTH + 1) // 2
N_C = DEPTH // 2
HA = 4
DVA = D_MODEL // (2 * HA)
DQA = DVA // 2
HB = 4
DKB = D_MODEL // (2 * HB)
DVB = DKB
HC = 8
KVC = 2
DHC = D_MODEL // HC
HI = 8
DI = 64
HX = 4
DHX = D_MODEL // HX
N_MEM = 256
D_FF = 256 * ((8 * D_MODEL // 3 + 255) // 256)
CONV_W = 3
ROPE_THETA = 500000.0
ROT_DIV = 4
Q_BLOCK = 128
HGRN_CHUNK = 64
DSA_TOPK_MAX = 256
EPS = 1e-6
AB_SPLITS = (2 * HA * DQA, 2 * HA * DQA, HA * DVA, HB * DKB, HB * DKB, HB * DVB, HB * DVB)
C_SPLITS = (HC * DHC, KVC * DHC, KVC * DHC, HI * DI, DI, HI)
AB_WIDTH = sum(AB_SPLITS)
C_WIDTH = sum(C_SPLITS)

kernel_name = 'hybrid_diffattn_hgrn2_dsa_decode_step'


def _rms(x, g):
    xf = x.astype(jnp.float32)
    y = xf * lax.rsqrt(jnp.mean(xf * xf, axis=-1, keepdims=True) + EPS)
    return (y * g.astype(jnp.float32)).astype(x.dtype)


def _split(z, sizes):
    return jnp.split(z, np.cumsum(sizes)[:-1].tolist(), axis=-1)


def _rope(x, pos):
    half = x.shape[-1] // ROT_DIV // 2
    rot = 2 * half
    inv = jnp.exp(jnp.arange(half, dtype=jnp.float32) * (-math.log(ROPE_THETA) / half))
    ang = pos.astype(jnp.float32)[:, None] * inv[None, :]
    cos = jnp.cos(ang)[None, :, None, :]
    sin = jnp.sin(ang)[None, :, None, :]
    x1 = x[..., :half].astype(jnp.float32)
    x2 = x[..., half:rot].astype(jnp.float32)
    xr = jnp.concatenate([x1 * cos - x2 * sin, x2 * cos + x1 * sin], axis=-1).astype(x.dtype)
    return jnp.concatenate([xr, x[..., rot:]], axis=-1)


def _gather_pages(pool, j, pt):
    g = pool[j, pt]
    return g.reshape(pt.shape[0], pt.shape[1] * pool.shape[2], *pool.shape[3:])


def _sweep_queries(fn, q_arrays, q_pos):
    T = q_pos.shape[0]
    if T <= Q_BLOCK or T % Q_BLOCK != 0:
        return fn(q_arrays, q_pos)
    nb = T // Q_BLOCK

    def split(a):
        return jnp.moveaxis(a.reshape(a.shape[0], nb, Q_BLOCK, *a.shape[2:]), 1, 0)

    blocks = tuple(split(a) for a in q_arrays)
    out = lax.map(lambda args: fn(args[0], args[1]), (blocks, q_pos.reshape(nb, Q_BLOCK)))
    out = jnp.moveaxis(out, 0, 1)
    return out.reshape(out.shape[0], T, *out.shape[3:])


def _diff_attention(q1, q2, k1, k2, v, q_pos, k_pos, lam):
    scale = DQA ** -0.5

    def blk(qs, qp):
        a, b = qs
        mask = (k_pos[None, :] <= qp[:, None])[None, None]

        def probs(q, k):
            s = jnp.einsum('bthd,bshd->bhts', q, k).astype(jnp.float32) * scale
            return jax.nn.softmax(jnp.where(mask, s, -jnp.inf), axis=-1)

        p = probs(a, k1) - lam * probs(b, k2)
        return jnp.einsum('bhts,bshd->bthd', p.astype(v.dtype), v)

    return _sweep_queries(blk, (q1, q2), q_pos)


def _hgrn2(q, k, v, logf, s0):
    B, T, H, _ = q.shape
    C = math.gcd(T, HGRN_CHUNK)
    n = T // C

    def chunks(a):
        return jnp.moveaxis(a.astype(jnp.float32).reshape(B, n, C, *a.shape[2:]), 1, 0)

    tri = jnp.tril(jnp.ones((C, C), dtype=bool))[None, :, :, None, None]

    def step(S, inp):
        qc, kc, vc, lc = inp
        b = jnp.cumsum(lc, axis=1)
        dec = jnp.exp(jnp.where(tri, b[:, :, None] - b[:, None, :], -jnp.inf))
        att = jnp.einsum('bthk,bshk,btshk->bhts', qc, kc, dec)
        o = jnp.einsum('bthk,bhkv->bthv', qc * jnp.exp(b), S) + jnp.einsum('bhts,bshv->bthv', att, vc)
        bl = b[:, -1]
        S = jnp.exp(bl)[..., None] * S + jnp.einsum('bshk,bshv->bhkv', kc * jnp.exp(bl[:, None] - b), vc)
        return S, o

    S, o = lax.scan(step, s0.astype(jnp.float32), (chunks(q), chunks(k), chunks(v), chunks(logf)))
    o = jnp.moveaxis(o, 0, 1).reshape(B, T, H, v.shape[-1])
    return o, S


def _ab_mixer(h, pos, l, j, W, past_kv, s0):
    B, T, _ = h.shape
    qa, ka, va, qb, fb, ib, gb = _split(h @ W['w_in_ab'][j], AB_SPLITS)
    qa = qa.reshape(B, T, HA, 2, DQA)
    ka = ka.reshape(B, T, HA, 2, DQA)
    q1 = _rope(qa[:, :, :, 0], pos)
    q2 = _rope(qa[:, :, :, 1], pos)
    k_rows = jnp.concatenate([_rope(ka[:, :, :, 0], pos), _rope(ka[:, :, :, 1], pos)], axis=-1)
    v_rows = va.reshape(B, T, HA, DVA)
    if past_kv is None:
        k_all, v_all = k_rows, v_rows
    else:
        k_all = jnp.concatenate([past_kv[0].astype(k_rows.dtype), k_rows], axis=1)
        v_all = jnp.concatenate([past_kv[1].astype(v_rows.dtype), v_rows], axis=1)
    lam_init = 0.8 - 0.6 * math.exp(-0.3 * l)
    f32 = jnp.float32
    lam = (jnp.exp(jnp.sum(W['diff_lq1'][j].astype(f32) * W['diff_lk1'][j].astype(f32)))
           - jnp.exp(jnp.sum(W['diff_lq2'][j].astype(f32) * W['diff_lk2'][j].astype(f32))) + lam_init)
    o_a = _diff_attention(q1, q2, k_all[..., :DQA], k_all[..., DQA:], v_all, pos,
                          jnp.arange(k_all.shape[1]), lam)
    o_a = _rms(o_a, W['diff_subln'][j]) * (1.0 - lam_init)
    lb = W['hgrn_lb'][j]
    fgate = lb + (1.0 - lb) * jax.nn.sigmoid(fb.astype(f32))

    def hd(a):
        return a.reshape(B, T, HB, a.shape[-1] // HB)

    qh = hd(jax.nn.silu(qb.astype(f32))) * DKB ** -0.5
    o_b, s_new = _hgrn2(qh, hd(1.0 - fgate), hd(ib.astype(f32)), hd(jnp.log(fgate)), s0)
    o_b = _rms(o_b.astype(h.dtype), W['hgrn_onorm'][j]) * jax.nn.silu(hd(gb))
    y = jnp.concatenate([o_a.reshape(B, T, HA * DVA), o_b.reshape(B, T, HB * DVB)], axis=-1) @ W['w_out_ab'][j]
    return y, k_rows, v_rows, s_new


def _dsa_attend(q, iq, iw, q_pos, ik_all, fetch, topk):
    k_pos = jnp.arange(ik_all.shape[1])
    scale = DHC ** -0.5

    def blk(qs, qp):
        qb, iqb, iwb = qs
        B, Tb = qb.shape[:2]
        isc = jax.nn.relu(jnp.einsum('bthd,bsd->bths', iqb, ik_all).astype(jnp.float32))
        isc = jnp.einsum('bths,bth->bts', isc, iwb.astype(jnp.float32))
        isc = jnp.where((k_pos[None, :] <= qp[:, None])[None], isc, -jnp.inf)
        top, sel = lax.top_k(isc, topk)
        ksel, vsel = fetch(sel)
        qg = qb.reshape(B, Tb, KVC, HC // KVC, DHC)
        s = jnp.einsum('btkgd,btjkd->btkgj', qg, ksel).astype(jnp.float32) * scale
        s = jnp.where(jnp.isfinite(top)[:, :, None, None, :], s, -jnp.inf)
        p = jax.nn.softmax(s, axis=-1).astype(vsel.dtype)
        return jnp.einsum('btkgj,btjkd->btkgd', p, vsel).reshape(B, Tb, HC, DHC)

    return _sweep_queries(blk, (q, iq, iw), q_pos)


def _c_mixer(h, pos, j, W, past):
    B, T, _ = h.shape
    q, k, v, iq, ik, iw = _split(h @ W['w_in_c'][j], C_SPLITS)
    q = _rope(q.reshape(B, T, HC, DHC), pos)
    k_rows = _rope(k.reshape(B, T, KVC, DHC), pos)
    v_rows = v.reshape(B, T, KVC, DHC)
    iq = _rope(iq.reshape(B, T, HI, DI), pos)
    ik_rows = _rope(_rms(ik, W['idx_k_norm'][j])[:, :, None, :], pos)[:, :, 0]
    iw = iw * (HI * DI) ** -0.5
    take = jax.vmap(lambda kb, vb, sb: (kb[sb], vb[sb]))
    if past is None:
        ik_all = ik_rows

        def fetch(sel):
            return take(k_rows, v_rows, sel)
    else:
        pool_k, pool_v, pool_ik, pt = past
        P = pt.shape[1] * PAGE_SIZE
        ik_all = jnp.concatenate([_gather_pages(pool_ik, j, pt).astype(ik_rows.dtype), ik_rows], axis=1)

        def fetch(sel):
            ps = jnp.minimum(sel, P - 1)
            phys = jax.vmap(lambda tb, sb: tb[sb])(pt, ps // PAGE_SIZE)
            slot = ps % PAGE_SIZE
            kn, vn = take(k_rows, v_rows, jnp.clip(sel - P, 0, T - 1))
            is_past = (sel < P)[..., None, None]
            return (jnp.where(is_past, pool_k[j, phys, slot].astype(kn.dtype), kn),
                    jnp.where(is_past, pool_v[j, phys, slot].astype(vn.dtype), vn))
    topk = min(DSA_TOPK_MAX, ik_all.shape[1] // 4)
    o = _dsa_attend(q, iq, iw, pos, ik_all, fetch, topk)
    y = o.reshape(B, T, HC * DHC) @ W['w_out_c'][j]
    return y, k_rows, v_rows, ik_rows


def _cross_attn(h, mk, mv, w_q, w_o):
    B, T, _ = h.shape
    q = (h @ w_q).reshape(B, T, HX, DHX)
    s = jnp.einsum('bthd,bmhd->bhtm', q, mk.astype(q.dtype)).astype(jnp.float32) * DHX ** -0.5
    p = jax.nn.softmax(s, axis=-1).astype(q.dtype)
    o = jnp.einsum('bhtm,bmhd->bthd', p, mv.astype(q.dtype)).reshape(B, T, HX * DHX)
    return o @ w_o


def _conv_ffn(h, l, W, prev):
    B, T, _ = h.shape
    g, u = jnp.split(h @ W['w_ffn_in'][l], 2, axis=-1)
    if prev is None:
        prev = jnp.zeros((B, CONV_W - 1, D_FF), g.dtype)
    gp = jnp.concatenate([prev.astype(g.dtype), g], axis=1)
    w = W['conv_ffn_w'][l]
    gc = W['conv_ffn_b'][l] + gp[:, 0:T] * w[0]
    for i in range(1, CONV_W):
        gc = gc + gp[:, i:i + T] * w[i]
    y = (jax.nn.silu(gc) * u) @ W['w_ffn_out'][l]
    return y, gp[:, T:]


def _trunk(x, pos, mem_k, mem_v, past, W):
    B = x.shape[0]
    new = {'diff_k': [], 'diff_v': [], 'hgrn': [], 'dsa_k': [], 'dsa_v': [], 'dsa_ik': [], 'conv': []}
    for l in range(DEPTH):
        j = l // 2
        h = _rms(x, W['norm_mix'][l])
        if l % 2 == 0:
            if past is None:
                past_kv = None
                s0 = jnp.zeros((B, HB, DKB, DVB), jnp.float32)
            else:
                past_kv = (_gather_pages(past['diff_k'], j, past['pt']), _gather_pages(past['diff_v'], j, past['pt']))
                s0 = past['hgrn'][j].astype(jnp.float32)
            y, k_rows, v_rows, s_new = _ab_mixer(h, pos, l, j, W, past_kv, s0)
            new['diff_k'].append(k_rows)
            new['diff_v'].append(v_rows)
            new['hgrn'].append(s_new)
        else:
            past_c = None if past is None else (past['dsa_k'], past['dsa_v'], past['dsa_ik'], past['pt'])
            y, k_rows, v_rows, ik_rows = _c_mixer(h, pos, j, W, past_c)
            new['dsa_k'].append(k_rows)
            new['dsa_v'].append(v_rows)
            new['dsa_ik'].append(ik_rows)
        x = x + y
        x = x + _cross_attn(_rms(x, W['norm_x'][l]), mem_k[l], mem_v[l], W['w_xq'][l], W['w_xo'][l])
        f, conv_state = _conv_ffn(_rms(x, W['norm_ffn'][l]), l, W, None if past is None else past['conv'][l])
        x = x + f
        new['conv'].append(conv_state)
    stacked = {name: jnp.stack(vals, axis=0) for name, vals in new.items()}
    return _rms(x, W['norm_final']), stacked


def setup_inputs(seed: int = 0) -> dict:
    key = jax.random.key(seed)
    ks = iter(jax.random.split(key, 64))
    f32 = jnp.float32

    def nrm(shape, scale=1.0):
        return scale * jax.random.normal(next(ks), shape, f32)

    def gain(shape):
        return 1.0 + 0.02 * jax.random.normal(next(ks), shape, f32)

    n_pages = PAST_LEN // PAGE_SIZE
    n_used = DEC_BATCH * n_pages
    n_pool = n_used + max(1, n_used // 4)
    page_table = jax.random.permutation(next(ks), n_pool)[:n_used].reshape(DEC_BATCH, n_pages).astype(jnp.int32)
    d = D_MODEL
    return {
        'x_prompt': nrm((BATCH, SEQ, d)),
        'x_sample': nrm((DEC_BATCH, DEC_SEQ, d)),
        'cache_diff_k': nrm((N_AB, n_pool, PAGE_SIZE, HA, 2 * DQA)),
        'cache_diff_v': nrm((N_AB, n_pool, PAGE_SIZE, HA, DVA)),
        'state_hgrn': nrm((N_AB, DEC_BATCH, HB, DKB, DVB), 0.5),
        'cache_dsa_k': nrm((N_C, n_pool, PAGE_SIZE, KVC, DHC)),
        'cache_dsa_v': nrm((N_C, n_pool, PAGE_SIZE, KVC, DHC)),
        'cache_dsa_ik': nrm((N_C, n_pool, PAGE_SIZE, DI)),
        'cache_mem_k': nrm((DEPTH, DEC_BATCH, N_MEM, HX, DHX)),
        'cache_mem_v': nrm((DEPTH, DEC_BATCH, N_MEM, HX, DHX)),
        'state_ffn_conv': nrm((DEPTH, DEC_BATCH, CONV_W - 1, D_FF)),
        'page_table': page_table,
        'mem_prompt': nrm((BATCH, N_MEM, d)),
        'norm_mix': gain((DEPTH, d)),
        'w_in_ab': nrm((N_AB, d, AB_WIDTH), d ** -0.5),
        'diff_lq1': nrm((N_AB, DQA), 0.1),
        'diff_lk1': nrm((N_AB, DQA), 0.1),
        'diff_lq2': nrm((N_AB, DQA), 0.1),
        'diff_lk2': nrm((N_AB, DQA), 0.1),
        'diff_subln': gain((N_AB, DVA)),
        'hgrn_lb_logits': nrm((N_AB + 1, HB * DKB)),
        'hgrn_onorm': gain((N_AB, DVB)),
        'w_out_ab': nrm((N_AB, HA * DVA + HB * DVB, d), (HA * DVA + HB * DVB) ** -0.5),
        'w_in_c': nrm((N_C, d, C_WIDTH), d ** -0.5),
        'idx_k_norm': gain((N_C, DI)),
        'w_out_c': nrm((N_C, HC * DHC, d), (HC * DHC) ** -0.5),
        'norm_x': gain((DEPTH, d)),
        'norm_mem': gain((DEPTH, d)),
        'w_xq': nrm((DEPTH, d, HX * DHX), d ** -0.5),
        'w_xk': nrm((DEPTH, d, HX * DHX), d ** -0.5),
        'w_xv': nrm((DEPTH, d, HX * DHX), d ** -0.5),
        'w_xo': nrm((DEPTH, HX * DHX, d), (HX * DHX) ** -0.5),
        'norm_ffn': gain((DEPTH, d)),
        'w_ffn_in': nrm((DEPTH, d, 2 * D_FF), d ** -0.5),
        'conv_ffn_w': nrm((DEPTH, CONV_W, D_FF), CONV_W ** -0.5),
        'conv_ffn_b': nrm((DEPTH, D_FF), 0.01),
        'w_ffn_out': nrm((DEPTH, D_FF, d), D_FF ** -0.5),
        'norm_final': gain((d,)),
    }


def reference(x_prompt, x_sample, cache_diff_k, cache_diff_v, state_hgrn, cache_dsa_k, cache_dsa_v, cache_dsa_ik,
              cache_mem_k, cache_mem_v, state_ffn_conv, page_table, mem_prompt,
              norm_mix, w_in_ab, diff_lq1, diff_lk1, diff_lq2, diff_lk2, diff_subln, hgrn_lb_logits, hgrn_onorm,
              w_out_ab, w_in_c, idx_k_norm, w_out_c, norm_x, norm_mem, w_xq, w_xk, w_xv, w_xo, norm_ffn,
              w_ffn_in, conv_ffn_w, conv_ffn_b, w_ffn_out, norm_final):
    hgrn_lb = jnp.cumsum(jax.nn.softmax(hgrn_lb_logits.astype(jnp.float32), axis=0), axis=0)[:N_AB]
    W = {
        'norm_mix': norm_mix, 'w_in_ab': w_in_ab, 'diff_lq1': diff_lq1, 'diff_lk1': diff_lk1,
        'diff_lq2': diff_lq2, 'diff_lk2': diff_lk2, 'diff_subln': diff_subln, 'hgrn_lb': hgrn_lb,
        'hgrn_onorm': hgrn_onorm, 'w_out_ab': w_out_ab, 'w_in_c': w_in_c, 'idx_k_norm': idx_k_norm,
        'w_out_c': w_out_c, 'norm_x': norm_x, 'w_xq': w_xq, 'w_xo': w_xo, 'norm_ffn': norm_ffn,
        'w_ffn_in': w_ffn_in, 'conv_ffn_w': conv_ffn_w, 'conv_ffn_b': conv_ffn_b, 'w_ffn_out': w_ffn_out,
        'norm_final': norm_final,
    }
    B, T, _ = x_prompt.shape
    Td = x_sample.shape[1]
    n_mem = mem_prompt.shape[1]
    mem_h = [_rms(mem_prompt, norm_mem[l]) for l in range(DEPTH)]
    mk_p = [(mem_h[l] @ w_xk[l]).reshape(B, n_mem, HX, DHX) for l in range(DEPTH)]
    mv_p = [(mem_h[l] @ w_xv[l]).reshape(B, n_mem, HX, DHX) for l in range(DEPTH)]
    y_prompt, sp = _trunk(x_prompt, jnp.arange(T), mk_p, mv_p, None, W)
    past_len = page_table.shape[1] * PAGE_SIZE
    past = {'diff_k': cache_diff_k, 'diff_v': cache_diff_v, 'hgrn': state_hgrn, 'dsa_k': cache_dsa_k,
            'dsa_v': cache_dsa_v, 'dsa_ik': cache_dsa_ik, 'conv': state_ffn_conv, 'pt': page_table}
    y_sample, ss = _trunk(x_sample, past_len + jnp.arange(Td), [cache_mem_k[l] for l in range(DEPTH)],
                          [cache_mem_v[l] for l in range(DEPTH)], past, W)
    return (y_prompt, y_sample,
            sp['diff_k'], sp['diff_v'], sp['hgrn'], sp['dsa_k'], sp['dsa_v'], sp['dsa_ik'],
            jnp.stack(mk_p, axis=0), jnp.stack(mv_p, axis=0), sp['conv'],
            ss['diff_k'], ss['diff_v'], ss['hgrn'], ss['dsa_k'], ss['dsa_v'], ss['dsa_ik'], ss['conv'])
```

```python
import functools
import math

import jax
import jax.numpy as jnp
import numpy as np
from jax import lax
from jax.experimental import pallas as pl
from jax.experimental.pallas import tpu as pltpu

F32 = jnp.float32
BF16 = jnp.bfloat16
I32 = jnp.int32

EPS = 1e-6
ROPE_THETA = 500000.0
ROT_DIV = 4
DSA_TOPK_MAX = 256
NEG = -1e30
INT_MIN = -2147483648
LANES = 128
VMEM_LIMIT = 56 * 1024 * 1024

NT = (((1,), (1,)), ((), ()))


def _cp(sem):
    return pltpu.CompilerParams(dimension_semantics=sem, vmem_limit_bytes=VMEM_LIMIT)


def _dot(a, b):
    return jnp.dot(a, b, preferred_element_type=F32)


def _dot_nt(a, b):
    return lax.dot_general(a, b, NT, preferred_element_type=F32)


def _rms(x, g):
    ms = jnp.mean(x * x, axis=-1, keepdims=True)
    return x * lax.rsqrt(ms + EPS) * g


def _silu(x):
    return x * jax.nn.sigmoid(x)


def _tile_lanes(t, n):
    return t if n == 1 else jnp.concatenate([t] * n, axis=1)


def _rope(z, c, s1, s2, half):
    w = z.shape[1]
    n = w // LANES
    return (z * _tile_lanes(c, n) + pltpu.roll(z, half, 1) * _tile_lanes(s1, n)
            + pltpu.roll(z, w - half, 1) * _tile_lanes(s2, n))


def _rope_tables(pos, head_dim):
    half = head_dim // ROT_DIV // 2
    inv = jnp.exp(jnp.arange(half, dtype=F32) * (-math.log(ROPE_THETA) / half))
    ang = pos.astype(F32)[:, None] * inv[None, :]
    cos, sin = jnp.cos(ang), jnp.sin(ang)
    m = pos.shape[0]
    rest = head_dim - 2 * half
    c = jnp.concatenate([cos, cos, jnp.ones((m, rest), F32)], axis=1)
    s1 = jnp.concatenate([jnp.zeros((m, half), F32), sin, jnp.zeros((m, rest), F32)], axis=1)
    s2 = jnp.concatenate([-sin, jnp.zeros((m, half + rest), F32)], axis=1)
    rep = LANES // head_dim
    return tuple(jnp.tile(t, (1, rep)) for t in (c, s1, s2)), half


def _const_spec(shape):
    nd = len(shape)
    return pl.BlockSpec(shape, lambda *_: (0,) * nd)


def _inproj_ab_kernel(x_ref, g_ref, w_ref, lb_ref, c_ref, s1_ref, s2_ref,
                      q_ref, kf_ref, vf_ref, kb_ref, vb_ref, qh_ref, kk_ref, lf_ref, ib_ref, gb_ref,
                      *, half, qscale, hscale):
    h = _rms(x_ref[...], g_ref[...]).astype(BF16)
    sw = q_ref.shape[1]

    def seg(i):
        return _dot(h, w_ref[:, i * sw:(i + 1) * sw])

    c, s1, s2 = c_ref[...], s1_ref[...], s2_ref[...]
    q_ref[...] = (_rope(seg(0), c, s1, s2, half) * qscale).astype(q_ref.dtype)
    ka = _rope(seg(1), c, s1, s2, half)
    kf_ref[...] = ka
    kb_ref[...] = ka.astype(BF16)
    va = seg(2)
    vf_ref[...] = va
    vb_ref[...] = va.astype(BF16)
    qh_ref[...] = _silu(seg(3)) * hscale
    lb = lb_ref[...]
    fg = lb + (1.0 - lb) * jax.nn.sigmoid(seg(4))
    kk_ref[...] = 1.0 - fg
    lf_ref[...] = jnp.log(fg)
    ib_ref[...] = seg(5)
    gb_ref[...] = _silu(seg(6))


def _inproj_ab(x, g, w, lb, tabs, half, dqa, dkb, qdtype, tm):
    m, d = x.shape
    sw = w.shape[1] // 7
    row = lambda width: pl.BlockSpec((tm, width), lambda i: (i, 0))
    outs = [(sw, qdtype), (sw, F32), (sw, F32), (sw, BF16), (sw, BF16)] + [(sw, F32)] * 5
    return pl.pallas_call(
        functools.partial(_inproj_ab_kernel, half=half, qscale=dqa ** -0.5, hscale=dkb ** -0.5),
        grid=(m // tm,),
        in_specs=[row(d), _const_spec((1, d)), _const_spec(w.shape), _const_spec((1, sw)),
                  row(LANES), row(LANES), row(LANES)],
        out_specs=[row(wd) for wd, _ in outs],
        out_shape=[jax.ShapeDtypeStruct((m, wd), dt) for wd, dt in outs],
        compiler_params=_cp(("arbitrary",)),
        name="inproj_ab",
    )(x, g.reshape(1, d), w, lb.reshape(1, sw), *tabs)


def _inproj_c_kernel(x_ref, g_ref, w_ref, ikg_ref, c128_ref, s1128_ref, s2128_ref, c64_ref, s164_ref, s264_ref,
                     q_ref, kf_ref, vf_ref, kb_ref, vb_ref, iq_ref, tail_ref, ik2_ref,
                     *, nq, nkv, niq, di, half128, half64, qscale, iwscale):
    h = _rms(x_ref[...], g_ref[...]).astype(BF16)
    c1, a1, b1 = c128_ref[...], s1128_ref[...], s2128_ref[...]
    c6, a6, b6 = c64_ref[...], s164_ref[...], s264_ref[...]
    o = 0
    q = _dot(h, w_ref[:, o:o + nq]); o += nq
    q_ref[...] = (_rope(q, c1, a1, b1, half128) * qscale).astype(q_ref.dtype)
    k = _rope(_dot(h, w_ref[:, o:o + nkv]), c1, a1, b1, half128); o += nkv
    kf_ref[...] = k
    kb_ref[...] = k.astype(BF16)
    v = _dot(h, w_ref[:, o:o + nkv]); o += nkv
    vf_ref[...] = v
    vb_ref[...] = v.astype(BF16)
    iq = _rope(_dot(h, w_ref[:, o:o + niq]), c6, a6, b6, half64); o += niq
    iq_ref[...] = iq.astype(iq_ref.dtype)
    t = _dot(h, w_ref[:, o:o + LANES])
    lane = lax.broadcasted_iota(I32, t.shape, 1)
    is_k = lane < di
    tk = jnp.where(is_k, t, 0.0)
    ms = jnp.sum(tk * tk, axis=-1, keepdims=True) * (1.0 / di)
    ikn = tk * lax.rsqrt(ms + EPS) * ikg_ref[...]
    ikr = _rope(ikn, c6, a6, b6, half64)
    ikr = jnp.where(is_k, ikr, 0.0)
    tail_ref[...] = jnp.where(is_k, ikr, t * iwscale)
    ik2_ref[...] = (ikr + pltpu.roll(ikr, di, 1)).astype(BF16)


def _inproj_c(x, g, w_pad, ikg_pad, tabs128, tabs64, half128, half64, dims, qdtype, tm):
    m, d = x.shape
    nq, nkv, niq, di, hi, dhc = dims
    row = lambda width: pl.BlockSpec((tm, width), lambda i: (i, 0))
    outs = [(nq, qdtype), (nkv, F32), (nkv, F32), (nkv, BF16), (nkv, BF16), (niq, qdtype), (LANES, F32), (LANES, BF16)]
    return pl.pallas_call(
        functools.partial(_inproj_c_kernel, nq=nq, nkv=nkv, niq=niq, di=di, half128=half128, half64=half64,
                          qscale=dhc ** -0.5, iwscale=(hi * di) ** -0.5),
        grid=(m // tm,),
        in_specs=[row(d), _const_spec((1, d)), _const_spec(w_pad.shape), _const_spec((1, LANES))]
                 + [row(LANES)] * 6,
        out_specs=[row(wd) for wd, _ in outs],
        out_shape=[jax.ShapeDtypeStruct((m, wd), dt) for wd, dt in outs],
        compiler_params=_cp(("arbitrary",)),
        name="inproj_c",
    )(x, g.reshape(1, d), w_pad, ikg_pad, *tabs128, *tabs64)


def _memproj_kernel(x_ref, g_ref, wk_ref, wv_ref, kf_ref, vf_ref, kb_ref, vb_ref):
    h = _rms(x_ref[...], g_ref[0]).astype(BF16)
    k = _dot(h, wk_ref[0])
    v = _dot(h, wv_ref[0])
    kf_ref[0] = k
    vf_ref[0] = v
    kb_ref[0] = k.astype(BF16)
    vb_ref[0] = v.astype(BF16)


def _memproj(x, g, wk, wv, tm):
    m, d = x.shape
    depth, _, n = wk.shape
    wspec = pl.BlockSpec((1, d, n), lambda l, i: (l, 0, 0))
    ospec = pl.BlockSpec((1, tm, n), lambda l, i: (l, i, 0))
    return pl.pallas_call(
        _memproj_kernel,
        grid=(depth, m // tm),
        in_specs=[pl.BlockSpec((tm, d), lambda l, i: (i, 0)), pl.BlockSpec((1, 1, d), lambda l, i: (l, 0, 0)),
                  wspec, wspec],
        out_specs=[ospec] * 4,
        out_shape=[jax.ShapeDtypeStruct((depth, m, n), dt) for dt in (F32, F32, BF16, BF16)],
        compiler_params=_cp(("arbitrary", "arbitrary")),
        name="memproj",
    )(x, g.reshape(depth, 1, d), wk, wv)


def _softmax_step(s, vb, m, l, a):
    m_new = jnp.maximum(m, jnp.max(s, axis=-1, keepdims=True))
    alpha = jnp.exp(m - m_new)
    p = jnp.exp(s - m_new)
    l = alpha * l + jnp.sum(p, axis=-1, keepdims=True)
    a = alpha * a + _dot(p.astype(BF16), vb)
    return m_new, l, a


def _diff_finish(a1, l1, a2, l2, lam, g, out_scale):
    o = a1 / l1 - lam * (a2 / l2)
    return _rms(o, g) * out_scale


def _diffattn_kernel(lam_ref, q_ref, k_ref, v_ref, g_ref, o_ref, *, tq, dq, out_scale):
    i = pl.program_id(2)
    q = q_ref[...]
    lane = lax.broadcasted_iota(I32, q.shape, 1)
    zero = jnp.zeros_like(q)
    q1 = jnp.where(lane < dq, q, zero)
    q2 = jnp.where(lane >= dq, q, zero)
    dv = v_ref.shape[1]

    def block(j, carry, mask):
        r0 = pl.multiple_of(j * tq, tq)
        kb = k_ref[pl.ds(r0, tq), :]
        vb = v_ref[pl.ds(r0, tq), :]
        s1 = _dot_nt(q1, kb)
        s2 = _dot_nt(q2, kb)
        if mask is not None:
            s1 = jnp.where(mask, s1, NEG)
            s2 = jnp.where(mask, s2, NEG)
        m1, l1, a1, m2, l2, a2 = carry
        m1, l1, a1 = _softmax_step(s1, vb, m1, l1, a1)
        m2, l2, a2 = _softmax_step(s2, vb, m2, l2, a2)
        return m1, l1, a1, m2, l2, a2

    def init():
        return (jnp.full((tq, 1), NEG, F32), jnp.zeros((tq, 1), F32), jnp.zeros((tq, dv), F32))

    carry = lax.fori_loop(0, i, lambda j, c: block(j, c, None), init() + init())
    causal = (lax.broadcasted_iota(I32, (tq, tq), 0) >= lax.broadcasted_iota(I32, (tq, tq), 1))
    m1, l1, a1, m2, l2, a2 = block(i, carry, causal)
    o_ref[...] = _diff_finish(a1, l1, a2, l2, lam_ref[0], g_ref[...], out_scale).astype(o_ref.dtype)


def _diffattn_prompt(lam, q, kb, vb, g, b, t, nh, dq, out_scale, tq):
    m, w = q.shape
    dv = w // nh
    nq = t // tq
    smem = pl.BlockSpec(memory_space=pltpu.SMEM)
    return pl.pallas_call(
        functools.partial(_diffattn_kernel, tq=tq, dq=dq, out_scale=out_scale),
        grid=(b, nh, nq),
        in_specs=[smem,
                  pl.BlockSpec((tq, dv), lambda bb, h, i: (bb * nq + i, h)),
                  pl.BlockSpec((t, dv), lambda bb, h, i: (bb, h)),
                  pl.BlockSpec((t, dv), lambda bb, h, i: (bb, h)),
                  _const_spec((1, dv))],
        out_specs=pl.BlockSpec((tq, dv), lambda bb, h, i: (bb * nq + i, h)),
        out_shape=jax.ShapeDtypeStruct((m, w), BF16),
        compiler_params=_cp(("arbitrary",) * 3),
        name="diffattn_prompt",
    )(lam.reshape(1), q, kb, vb, g.reshape(1, dv))


def _pad_rows(x, rows):
    return jnp.concatenate([x, jnp.zeros((rows - x.shape[0], x.shape[1]), x.dtype)], axis=0)


def _diff_decode_kernel(pt_ref, lam_ref, q_ref, kn_ref, vn_ref, g_ref, *rest, G, nh, dq, out_scale, page):
    kp, vp = rest[:G], rest[G:2 * G]
    o_ref, m_ref, l_ref, a_ref = rest[2 * G:]
    p = pl.program_id(1)
    td = q_ref.shape[0]
    dv = q_ref.shape[1] // nh

    @pl.when(p == 0)
    def _():
        m_ref[...] = jnp.full(m_ref.shape, NEG, F32)
        l_ref[...] = jnp.zeros(l_ref.shape, F32)
        a_ref[...] = jnp.zeros(a_ref.shape, F32)

    q = q_ref[...].astype(BF16)
    lane = lax.broadcasted_iota(I32, (td, dv), 1)

    def q_pair(h):
        qh = q[:, h * dv:(h + 1) * dv]
        zero = jnp.zeros_like(qh)
        return jnp.concatenate([jnp.where(lane < dq, qh, zero), jnp.where(lane >= dq, qh, zero)], axis=0)

    def update(h, s, vlist):
        m_old = m_ref[h]
        m_new = jnp.maximum(m_old, jnp.max(s, axis=-1, keepdims=True))
        alpha = jnp.exp(m_old - m_new)
        pm = jnp.exp(s - m_new[:, :1]).astype(BF16)
        l_ref[h] = alpha * l_ref[h] + jnp.sum(pm.astype(F32), axis=-1, keepdims=True)
        pv = jnp.zeros((2 * td, dv), F32)
        for r, vb in enumerate(vlist):
            pv = pv + _dot(pm[:, r * page:(r + 1) * page], vb)
        a_ref[h] = alpha * a_ref[h] + pv
        m_ref[h] = m_new

    for h in range(nh):
        q2 = q_pair(h)
        s = jnp.concatenate([_dot_nt(q2, kp[r][0, :, h * dv:(h + 1) * dv].astype(BF16)) for r in range(G)], axis=1)
        update(h, s, [vp[r][0, :, h * dv:(h + 1) * dv].astype(BF16) for r in range(G)])

    @pl.when(p == pl.num_programs(1) - 1)
    def _():
        row = lax.broadcasted_iota(I32, (2 * td, page), 0)
        col = lax.broadcasted_iota(I32, (2 * td, page), 1)
        tt = jnp.where(row >= td, row - td, row)
        valid = col <= tt
        kn = _pad_rows(kn_ref[...], page).astype(BF16)
        vn = _pad_rows(vn_ref[...], page).astype(BF16)
        outs = []
        for h in range(nh):
            s = jnp.where(valid, _dot_nt(q_pair(h), kn[:, h * dv:(h + 1) * dv]), NEG)
            update(h, s, [vn[:, h * dv:(h + 1) * dv]])
            a, l = a_ref[h], l_ref[h]
            outs.append(_diff_finish(a[:td], l[:td], a[td:], l[td:], lam_ref[0], g_ref[...], out_scale))
        o_ref[...] = jnp.concatenate(outs, axis=1)


def _diff_decode(pt, lam, q, kn, vn, g, pool_k, pool_v, bd, td, nh, dq, out_scale, G):
    m, w = q.shape
    dv = w // nh
    npg = pt.shape[1] // G
    page = pool_k.shape[1]
    seq = pl.BlockSpec((td, w), lambda s, p, pt_: (s, 0))

    def page_spec(r):
        return pl.BlockSpec((1, page, w), lambda s, p, pt_: (pt_[s, p * G + r], 0, 0))

    grid_spec = pltpu.PrefetchScalarGridSpec(
        num_scalar_prefetch=1,
        grid=(bd, npg),
        in_specs=[pl.BlockSpec(memory_space=pltpu.SMEM), seq, seq, seq,
                  pl.BlockSpec((1, dv), lambda s, p, pt_: (0, 0))]
                 + [page_spec(r) for r in range(G)] * 2,
        out_specs=seq,
        scratch_shapes=[pltpu.VMEM((nh, 2 * td, LANES), F32), pltpu.VMEM((nh, 2 * td, LANES), F32),
                        pltpu.VMEM((nh, 2 * td, dv), F32)],
    )
    return pl.pallas_call(
        functools.partial(_diff_decode_kernel, G=G, nh=nh, dq=dq, out_scale=out_scale, page=page),
        grid_spec=grid_spec,
        out_shape=jax.ShapeDtypeStruct((m, w), F32),
        compiler_params=_cp(("arbitrary", "arbitrary")),
        name="diffattn_decode",
    )(pt, lam.reshape(1), q, kn, vn, g.reshape(1, dv), *([pool_k] * G), *([pool_v] * G))


def _cumsum_rows(x):
    n = x.shape[0]
    row = lax.broadcasted_iota(I32, x.shape, 0)
    s = 1
    while s < n:
        x = x + jnp.where(row >= s, pltpu.roll(x, s, 0), 0.0)
        s *= 2
    return x


def _hgrn_chunk(q, k, v, lf, st, sub):
    c = q.shape[0]
    b = _cumsum_rows(lf)
    o = _dot_nt((q * jnp.exp(b)).astype(BF16), st.astype(BF16))
    row_c = lax.broadcasted_iota(I32, (c, q.shape[1]), 0)
    row_s = lax.broadcasted_iota(I32, (sub, q.shape[1]), 0)
    vb = v.astype(BF16)
    parts = []
    for i in range(c // sub):
        lo = i * sub
        qi, ki, vi, bi = (a[lo:lo + sub] for a in (q, k, v, b))
        acc = jnp.zeros((sub, v.shape[1]), F32)
        for s in range(sub):
            d = jnp.where(row_s >= s, bi - bi[s:s + 1], -jnp.inf)
            a = jnp.sum(jnp.exp(d) * qi * ki[s:s + 1], axis=-1, keepdims=True)
            acc = acc + a * vi[s:s + 1]
        if i > 0:
            r = b[lo - 1:lo]
            qs = (qi * jnp.exp(bi - r)).astype(BF16)
            ks = (k * jnp.exp(jnp.where(row_c < lo, r - b, -jnp.inf))).astype(BF16)
            acc = acc + _dot(_dot_nt(qs, ks).astype(BF16), vb)
        parts.append(acc)
    o = o + (parts[0] if len(parts) == 1 else jnp.concatenate(parts, axis=0))
    bl = b[c - 1:c]
    kd = k * jnp.exp(bl - b)
    if c % LANES:
        pad = -c % LANES
        v, kd = _pad_rows(v, c + pad), _pad_rows(kd, c + pad)
    st = st * jnp.exp(bl) + _dot(v.T.astype(BF16), kd.astype(BF16))
    return o, st


def _hgrn_kernel(*refs, c, sub, has_s0):
    if has_s0:
        q_ref, k_ref, v_ref, lf_ref, gb_ref, on_ref, s0_ref, o_ref, so_ref, st_ref = refs
    else:
        q_ref, k_ref, v_ref, lf_ref, gb_ref, on_ref, o_ref, so_ref, st_ref = refs
    i = pl.program_id(2)

    @pl.when(i == 0)
    def _():
        st_ref[...] = s0_ref[0, 0].T if has_s0 else jnp.zeros(st_ref.shape, F32)

    def body(ci, carry):
        r0 = pl.multiple_of(ci * c, c)
        sl = pl.ds(r0, c)
        o, st = _hgrn_chunk(q_ref[sl, :], k_ref[sl, :], v_ref[sl, :], lf_ref[sl, :], st_ref[...], sub)
        st_ref[...] = st
        o_ref[sl, :] = (_rms(o, on_ref[...]) * gb_ref[sl, :]).astype(o_ref.dtype)
        return carry

    lax.fori_loop(0, q_ref.shape[0] // c, body, 0)

    @pl.when(i == pl.num_programs(2) - 1)
    def _():
        so_ref[0, 0] = st_ref[...].T


def _hgrn(qh, kk, ib, lf, gb, onorm, s0, b, t, nh, tc, c, sub, odtype):
    m, w = qh.shape
    dk = w // nh
    nt = t // tc
    blk = pl.BlockSpec((tc, dk), lambda bb, h, i: (bb * nt + i, h))
    st_spec = pl.BlockSpec((1, 1, dk, dk), lambda bb, h, i: (bb, h, 0, 0))
    ins = [qh, kk, ib, lf, gb, onorm.reshape(1, dk)]
    in_specs = [blk] * 5 + [_const_spec((1, dk))]
    if s0 is not None:
        ins.append(s0)
        in_specs.append(st_spec)
    return pl.pallas_call(
        functools.partial(_hgrn_kernel, c=c, sub=sub, has_s0=s0 is not None),
        grid=(b, nh, nt),
        in_specs=in_specs,
        out_specs=[blk, st_spec],
        out_shape=[jax.ShapeDtypeStruct((m, w), odtype), jax.ShapeDtypeStruct((b, nh, dk, dk), F32)],
        scratch_shapes=[pltpu.VMEM((dk, dk), F32)],
        compiler_params=_cp(("arbitrary",) * 3),
        name="hgrn",
    )(*ins)


def _mixer_out(x, parts):
    for a_ref, w_ref in parts:
        x = x + _dot(a_ref[...].astype(BF16), w_ref[...])
    return x


def _mem_attend(q, mk, mv, nh):
    dh = q.shape[1] // nh
    outs = []
    for h in range(nh):
        sl = slice(h * dh, (h + 1) * dh)
        s = _dot_nt(q[:, sl], mk[:, sl])
        p = jnp.exp(s - jnp.max(s, axis=-1, keepdims=True))
        p = p / jnp.sum(p, axis=-1, keepdims=True)
        outs.append(_dot(p.astype(BF16), mv[:, sl]))
    return jnp.concatenate(outs, axis=1).astype(BF16)


def _cross_prompt_kernel(*refs, n_parts, nh, qscale):
    x_ref = refs[0]
    parts = [(refs[1 + 2 * j], refs[2 + 2 * j]) for j in range(n_parts)]
    g_ref, wq_ref, wo_ref, mk_ref, mv_ref, o_ref = refs[1 + 2 * n_parts:]
    x1 = _mixer_out(x_ref[...], parts)
    h = _rms(x1, g_ref[...]).astype(BF16)
    q = (_dot(h, wq_ref[...]) * qscale).astype(BF16)
    o = _mem_attend(q, mk_ref[0], mv_ref[0], nh)
    o_ref[...] = x1 + _dot(o, wo_ref[...])


def _cross_prompt(x, parts, g, wq, wo, mk, mv, nh, t, tm):
    m, d = x.shape
    n_mem = mk.shape[1]
    per = t // tm
    row = lambda width: pl.BlockSpec((tm, width), lambda i: (i, 0))
    mem = pl.BlockSpec((1, n_mem, d), lambda i: (i // per, 0, 0))
    ins, in_specs = [x], [row(d)]
    for a, w in parts:
        ins += [a, w]
        in_specs += [row(a.shape[1]), _const_spec(w.shape)]
    ins += [g.reshape(1, d), wq, wo, mk, mv]
    in_specs += [_const_spec((1, d)), _const_spec(wq.shape), _const_spec(wo.shape), mem, mem]
    return pl.pallas_call(
        functools.partial(_cross_prompt_kernel, n_parts=len(parts), nh=nh, qscale=(d // nh) ** -0.5),
        grid=(m // tm,),
        in_specs=in_specs,
        out_specs=row(d),
        out_shape=jax.ShapeDtypeStruct((m, d), F32),
        compiler_params=_cp(("arbitrary",)),
        name="cross_prompt",
    )(*ins)


def _cross_sample_kernel(*refs, n_parts, nh, qscale, td):
    x_ref = refs[0]
    parts = [(refs[1 + 2 * j], refs[2 + 2 * j]) for j in range(n_parts)]
    g_ref, wq_ref, wo_ref, mk_ref, mv_ref, o_ref, x1_ref, q_ref, a_ref = refs[1 + 2 * n_parts:]
    s = pl.program_id(0)

    @pl.when(s == 0)
    def _():
        x1 = _mixer_out(x_ref[...], parts)
        x1_ref[...] = x1
        q_ref[...] = _dot(_rms(x1, g_ref[...]).astype(BF16), wq_ref[...]) * qscale

    sl = pl.ds(pl.multiple_of(s * td, td), td)
    a_ref[sl, :] = _mem_attend(q_ref[sl, :].astype(BF16), mk_ref[0].astype(BF16), mv_ref[0].astype(BF16),
                               nh).astype(F32)

    @pl.when(s == pl.num_programs(0) - 1)
    def _():
        o_ref[...] = x1_ref[...] + _dot(a_ref[...].astype(BF16), wo_ref[...])


def _cross_sample(x, parts, g, wq, wo, mk, mv, nh, bd, td):
    m, d = x.shape
    n_mem = mk.shape[1]
    mem = pl.BlockSpec((1, n_mem, d), lambda s: (s, 0, 0))
    ins, in_specs = [x], [_const_spec((m, d))]
    for a, w in parts:
        ins += [a, w]
        in_specs += [_const_spec(a.shape), _const_spec(w.shape)]
    ins += [g.reshape(1, d), wq, wo, mk, mv]
    in_specs += [_const_spec((1, d)), _const_spec(wq.shape), _const_spec(wo.shape), mem, mem]
    return pl.pallas_call(
        functools.partial(_cross_sample_kernel, n_parts=len(parts), nh=nh, qscale=(d // nh) ** -0.5, td=td),
        grid=(bd,),
        in_specs=in_specs,
        out_specs=_const_spec((m, d)),
        out_shape=jax.ShapeDtypeStruct((m, d), F32),
        scratch_shapes=[pltpu.VMEM((m, d), F32)] * 3,
        compiler_params=_cp(("arbitrary",)),
        name="cross_sample",
    )(*ins)


def _ffn_kernel(*refs, carry_mode, seq_rows, blocks_per_seq, nchunk, cw, final_norm):
    if carry_mode:
        (x_ref, g_ref, wi_ref, cwt_ref, cb_ref, wo_ref, gf_ref, o_ref, st_ref, prev_ref) = refs
    else:
        (x_ref, g_ref, wi_ref, cwt_ref, cb_ref, wo_ref, gf_ref, f1_ref, f2_ref, o_ref, gate_ref) = refs
    x = x_ref[...]
    tm = x.shape[0]
    dff = cwt_ref.shape[1]
    h = _rms(x, g_ref[...]).astype(BF16)
    row = lax.broadcasted_iota(I32, (tm, cw), 0)
    if carry_mode:
        @pl.when(pl.program_id(0) % blocks_per_seq == 0)
        def _():
            prev_ref[...] = jnp.zeros(prev_ref.shape, F32)
    else:
        rseq = row % seq_rows
    acc = jnp.zeros(x.shape, F32)
    for ci in range(nchunk):
        c0 = ci * cw
        cs = slice(c0, c0 + cw)
        gt = _dot(h, wi_ref[:, cs])
        u = _dot(h, wi_ref[:, dff + c0:dff + c0 + cw])
        if carry_mode:
            p0, p1 = prev_ref[0:1, cs], prev_ref[1:2, cs]
            g1 = jnp.where(row == 0, p1, pltpu.roll(gt, 1, 0))
            g2 = jnp.where(row == 0, p0, jnp.where(row == 1, p1, pltpu.roll(gt, 2, 0)))
            last = gt[tm - 2:tm]
            prev_ref[:, cs] = last
            st_ref[0, :, cs] = last
        else:
            g1 = jnp.where(rseq >= 1, pltpu.roll(gt, 1, 0), f1_ref[:, cs])
            g2 = jnp.where(rseq >= 2, pltpu.roll(gt, 2, 0), f2_ref[:, cs])
            gate_ref[:, cs] = gt
        gc = cb_ref[:, cs] + g2 * cwt_ref[0:1, cs] + g1 * cwt_ref[1:2, cs] + gt * cwt_ref[2:3, cs]
        acc = acc + _dot((_silu(gc) * u).astype(BF16), wo_ref[cs, :])
    y = x + acc
    if final_norm:
        y = _rms(y, gf_ref[...])
    o_ref[...] = y


def _ffn(x, g, wi, cwt, cb, wo, gfinal, final_norm, seq_rows, tm, fills=None):
    m, d = x.shape
    dff = wo.shape[0]
    cw = 256
    carry_mode = fills is None
    nseq = m // seq_rows
    row = lambda width: pl.BlockSpec((tm, width), lambda i: (i, 0))
    ins = [x, g.reshape(1, d), wi, cwt, cb.reshape(1, dff), wo, gfinal.reshape(1, d)]
    in_specs = [row(d), _const_spec((1, d)), _const_spec(wi.shape), _const_spec(cwt.shape),
                _const_spec((1, dff)), _const_spec(wo.shape), _const_spec((1, d))]
    if carry_mode:
        per = seq_rows // tm
        out_specs = [row(d), pl.BlockSpec((1, 2, dff), lambda i: (i // per, 0, 0))]
        out_shape = [jax.ShapeDtypeStruct((m, d), F32), jax.ShapeDtypeStruct((nseq, 2, dff), F32)]
        scratch = [pltpu.VMEM((2, dff), F32)]
    else:
        per = 1
        ins += list(fills)
        in_specs += [row(dff), row(dff)]
        out_specs = [row(d), row(dff)]
        out_shape = [jax.ShapeDtypeStruct((m, d), F32), jax.ShapeDtypeStruct((m, dff), F32)]
        scratch = []
    return pl.pallas_call(
        functools.partial(_ffn_kernel, carry_mode=carry_mode, seq_rows=seq_rows, blocks_per_seq=per,
                          nchunk=dff // cw, cw=cw, final_norm=final_norm),
        grid=(m // tm,),
        in_specs=in_specs,
        out_specs=out_specs,
        out_shape=out_shape,
        scratch_shapes=scratch,
        compiler_params=_cp(("arbitrary",)),
        name="ffn",
    )(*ins)


def _sort_key(x):
    bits = lax.bitcast_convert_type(x, I32)
    key = jnp.where(bits < 0, bits ^ 0x7FFFFFFF, bits)
    return jnp.where(x == 0.0, 0, key)


def _fold_lanes(c):
    parts = [c[:, j * LANES:(j + 1) * LANES] for j in range(c.shape[1] // LANES)]
    while len(parts) > 1:
        parts = [parts[j] + parts[j + 1] for j in range(0, len(parts) - 1, 2)] + (parts[-1:] if len(parts) % 2 else [])
    return parts[0]


def _select_topk(keys_ref, n_tiles, topk, idx_bits):
    _, rows, tw = keys_ref.shape
    lane = lax.broadcasted_iota(I32, (rows, tw), 1)

    def count(ind):
        def body(j, acc):
            return acc + _fold_lanes(ind(keys_ref[j], j))
        acc = lax.fori_loop(0, n_tiles, body, jnp.zeros((rows, LANES), F32))
        return jnp.sum(acc, axis=-1, keepdims=True)

    def thr_step(it, thr):
        trial = thr ^ jnp.left_shift(jnp.int32(1), 31 - it)
        cnt = count(lambda kt, j: jnp.where(kt >= trial, 1.0, 0.0))
        return jnp.where(cnt >= topk, trial, thr)

    thr = lax.fori_loop(0, 32, thr_step, jnp.full((rows, 1), INT_MIN, I32))
    thr = jnp.maximum(thr, INT_MIN + 1)
    need = topk - count(lambda kt, j: jnp.where(kt > thr, 1.0, 0.0))

    def cut_step(it, cut):
        trial = cut | jnp.left_shift(jnp.int32(1), idx_bits - 1 - it)
        cnt = count(lambda kt, j: jnp.where(kt == thr, jnp.where(lane + j * tw < trial, 1.0, 0.0), 0.0))
        return jnp.where(cnt <= need, trial, cut)

    cut = lax.fori_loop(0, idx_bits, cut_step, jnp.zeros((rows, 1), I32))
    return thr, cut


def _selection_bias(kt, j, tw, thr, cut):
    lane = lax.broadcasted_iota(I32, kt.shape, 1)
    return jnp.where(kt > thr, 0.0, jnp.where(kt == thr, jnp.where(lane + j * tw < cut, 0.0, NEG), NEG))


def _index_scores(iq, iw, ik2_tile, nhi, di):
    lane = lax.broadcasted_iota(I32, (iq.shape[0], LANES), 1)
    acc = None
    for pr in range(nhi * di // LANES):
        pair = iq[:, pr * LANES:(pr + 1) * LANES]
        zero = jnp.zeros_like(pair)
        for half_i, qm in enumerate((jnp.where(lane < di, pair, zero), jnp.where(lane >= di, pair, zero))):
            hd = 2 * pr + half_i
            term = jnp.maximum(_dot_nt(qm, ik2_tile), 0.0) * iw[:, hd:hd + 1]
            acc = term if acc is None else acc + term
    return acc


def _dsa_index_kernel(iq_ref, tail_ref, ik2_ref, bias_ref, keys_ref, *, tq, tw, topk, nhi, di, idx_bits):
    i = pl.program_id(1)
    nt_all = keys_ref.shape[0]
    n_tiles = (i * tq + tq + tw - 1) // tw
    iq = iq_ref[...]
    iw = tail_ref[:, di:di + nhi]
    qpos = i * tq + lax.broadcasted_iota(I32, (tq, tw), 0)
    lane = lax.broadcasted_iota(I32, (tq, tw), 1)

    def score_tile(j, c):
        r0 = pl.multiple_of(j * tw, tw)
        sc = _index_scores(iq, iw, ik2_ref[pl.ds(r0, tw), :], nhi, di)
        keys_ref[j] = jnp.where(lane + j * tw <= qpos, _sort_key(sc), INT_MIN)
        return c

    lax.fori_loop(0, n_tiles, score_tile, 0)
    thr, cut = _select_topk(keys_ref, n_tiles, topk, idx_bits)

    def write_tile(j, c):
        bias_ref[0, j] = _selection_bias(keys_ref[j], j, tw, thr, cut).astype(bias_ref.dtype)
        return c

    lax.fori_loop(0, n_tiles, write_tile, 0)

    def fill_tile(j, c):
        bias_ref[0, j] = jnp.full((tq, tw), NEG, bias_ref.dtype)
        return c

    lax.fori_loop(n_tiles, nt_all, fill_tile, 0)


def _dsa_index_prompt(iq, tail, ik2, b, t, topk, nhi, di, tq, tw):
    m = iq.shape[0]
    nq = t // tq
    nt = t // tw
    return pl.pallas_call(
        functools.partial(_dsa_index_kernel, tq=tq, tw=tw, topk=topk, nhi=nhi, di=di,
                          idx_bits=max(1, t.bit_length())),
        grid=(b, nq),
        in_specs=[pl.BlockSpec((tq, iq.shape[1]), lambda bb, i: (bb * nq + i, 0)),
                  pl.BlockSpec((tq, LANES), lambda bb, i: (bb * nq + i, 0)),
                  pl.BlockSpec((t, LANES), lambda bb, i: (bb, 0))],
        out_specs=pl.BlockSpec((1, nt, tq, tw), lambda bb, i: (bb * nq + i, 0, 0, 0)),
        out_shape=jax.ShapeDtypeStruct((m // tq, nt, tq, tw), BF16),
        scratch_shapes=[pltpu.VMEM((nt, tq, tw), I32)],
        compiler_params=_cp(("arbitrary", "arbitrary")),
        name="dsa_index",
    )(iq, tail, ik2)


def _gqa_update(g, s, vb, m_ref, l_ref, a_ref):
    m_old = m_ref[g]
    m_new = jnp.maximum(m_old, jnp.max(s, axis=-1, keepdims=True))
    alpha = jnp.exp(m_old - m_new)
    p = jnp.exp(s - m_new[:, :1])
    l_ref[g] = alpha * l_ref[g] + jnp.sum(p, axis=-1, keepdims=True)
    a_ref[g] = alpha * a_ref[g] + _dot(p.astype(BF16), vb)
    m_ref[g] = m_new


def _stack_heads(q, g, per, dh):
    return jnp.concatenate([q[:, (g * per + hl) * dh:(g * per + hl + 1) * dh] for hl in range(per)], axis=0)


def _unstack_heads(a_ref, l_ref, nkv, per, rows):
    outs = []
    for g in range(nkv):
        o = a_ref[g] / l_ref[g][:, :1]
        outs += [o[hl * rows:(hl + 1) * rows] for hl in range(per)]
    return jnp.concatenate(outs, axis=1)


def _dsa_attn_kernel(q_ref, k_ref, v_ref, bias_ref, o_ref, m_ref, l_ref, a_ref, *, tq, tw, nkv, per, dh):
    i = pl.program_id(1)
    n_tiles = (i * tq + tq + tw - 1) // tw
    m_ref[...] = jnp.full(m_ref.shape, NEG, F32)
    l_ref[...] = jnp.zeros(l_ref.shape, F32)
    a_ref[...] = jnp.zeros(a_ref.shape, F32)
    q = q_ref[...]
    qg = [_stack_heads(q, g, per, dh) for g in range(nkv)]

    def body(j, c):
        r0 = pl.multiple_of(j * tw, tw)
        bias = bias_ref[0, j].astype(F32)
        bias = jnp.concatenate([bias] * per, axis=0)
        for g in range(nkv):
            kb = k_ref[pl.ds(r0, tw), g * dh:(g + 1) * dh]
            vb = v_ref[pl.ds(r0, tw), g * dh:(g + 1) * dh]
            _gqa_update(g, _dot_nt(qg[g], kb) + bias, vb, m_ref, l_ref, a_ref)
        return c

    lax.fori_loop(0, n_tiles, body, 0)
    o_ref[...] = _unstack_heads(a_ref, l_ref, nkv, per, tq).astype(o_ref.dtype)


def _dsa_attn_prompt(q, kb, vb, bias, b, t, nkv, tq, tw):
    m, w = q.shape
    dh = kb.shape[1] // nkv
    per = w // dh // nkv
    nq = t // tq
    nt = t // tw
    return pl.pallas_call(
        functools.partial(_dsa_attn_kernel, tq=tq, tw=tw, nkv=nkv, per=per, dh=dh),
        grid=(b, nq),
        in_specs=[pl.BlockSpec((tq, w), lambda bb, i: (bb * nq + i, 0)),
                  pl.BlockSpec((t, nkv * dh), lambda bb, i: (bb, 0)),
                  pl.BlockSpec((t, nkv * dh), lambda bb, i: (bb, 0)),
                  pl.BlockSpec((1, nt, tq, tw), lambda bb, i: (bb * nq + i, 0, 0, 0))],
        out_specs=pl.BlockSpec((tq, w), lambda bb, i: (bb * nq + i, 0)),
        out_shape=jax.ShapeDtypeStruct((m, w), BF16),
        scratch_shapes=[pltpu.VMEM((nkv, per * tq, LANES), F32), pltpu.VMEM((nkv, per * tq, LANES), F32),
                        pltpu.VMEM((nkv, per * tq, dh), F32)],
        compiler_params=_cp(("arbitrary", "arbitrary")),
        name="dsa_attn",
    )(q, kb, vb, bias)


def _dsa_dec_scores_kernel(pt_ref, iq_ref, iw_ref, ikn_ref, *rest, G, td, nhi, page, wnew):
    ikp = rest[:G]
    past_ref, new_ref = rest[G:]
    iq = iq_ref[0].astype(BF16)
    iw = iw_ref[0]

    def scores(ik):
        s = jnp.maximum(_dot_nt(iq, ik), 0.0) * iw
        out = s[0:td]
        for h in range(1, nhi):
            out = out + s[h * td:(h + 1) * td]
        return out

    past_ref[0] = jnp.concatenate([scores(ikp[r][0].astype(BF16)) for r in range(G)], axis=1)

    @pl.when(pl.program_id(1) == pl.num_programs(1) - 1)
    def _():
        sn = scores(_pad_rows(ikn_ref[...], wnew).astype(BF16))
        row = lax.broadcasted_iota(I32, sn.shape, 0)
        col = lax.broadcasted_iota(I32, sn.shape, 1)
        new_ref[0] = jnp.where(col <= row, sn, -jnp.inf)


def _dsa_dec_scores(pt, iq_hm, iw_hm, ik_new, pool_ik, bd, td, nhi, G, wnew):
    di = pool_ik.shape[2]
    page = pool_ik.shape[1]
    npg = pt.shape[1] // G
    seq3 = lambda shape: pl.BlockSpec((1,) + shape, lambda s, p, pt_: (s, 0, 0))
    grid_spec = pltpu.PrefetchScalarGridSpec(
        num_scalar_prefetch=1,
        grid=(bd, npg),
        in_specs=[seq3((nhi * td, di)), seq3((nhi * td, 1)), pl.BlockSpec((td, di), lambda s, p, pt_: (s, 0))]
                 + [pl.BlockSpec((1, page, di), functools.partial(lambda s, p, pt_, r: (pt_[s, p * G + r], 0, 0), r=r))
                    for r in range(G)],
        out_specs=[pl.BlockSpec((1, td, G * page), lambda s, p, pt_: (s, 0, p)), seq3((td, wnew))],
    )
    return pl.pallas_call(
        functools.partial(_dsa_dec_scores_kernel, G=G, td=td, nhi=nhi, page=page, wnew=wnew),
        grid_spec=grid_spec,
        out_shape=[jax.ShapeDtypeStruct((bd, td, pt.shape[1] * page), F32), jax.ShapeDtypeStruct((bd, td, wnew), F32)],
        compiler_params=_cp(("arbitrary", "arbitrary")),
        name="dsa_dec_scores",
    )(pt, iq_hm, iw_hm, ik_new, *([pool_ik] * G))


def _dsa_dec_select_kernel(sc_ref, bias_ref, keys_ref, *, topk, idx_bits):
    nt, rows, tw = keys_ref.shape
    for j in range(nt):
        sc = sc_ref[j]
        keys_ref[j] = jnp.where(sc == -jnp.inf, INT_MIN, _sort_key(sc))
    thr, cut = _select_topk(keys_ref, nt, topk, idx_bits)
    for j in range(nt):
        bias_ref[j] = _selection_bias(keys_ref[j], j, tw, thr, cut)


def _dsa_dec_select(sc_tiles, topk, rows):
    nt, m, tw = sc_tiles.shape
    spec = pl.BlockSpec((nt, rows, tw), lambda i: (0, i, 0))
    return pl.pallas_call(
        functools.partial(_dsa_dec_select_kernel, topk=topk, idx_bits=max(1, (nt * tw).bit_length())),
        grid=(m // rows,),
        in_specs=[spec],
        out_specs=spec,
        out_shape=jax.ShapeDtypeStruct((nt, m, tw), F32),
        scratch_shapes=[pltpu.VMEM((nt, rows, tw), I32)],
        compiler_params=_cp(("arbitrary",)),
        name="dsa_dec_select",
    )(sc_tiles)


def _dsa_dec_attn_kernel(pt_ref, q_ref, kn_ref, vn_ref, bias_ref, bnew_ref, *rest, G, td, nkv, per, dh, page):
    kp, vp = rest[:G], rest[G:2 * G]
    o_ref, m_ref, l_ref, a_ref = rest[2 * G:]
    p = pl.program_id(1)
    tw = bias_ref.shape[2]

    @pl.when(p == 0)
    def _():
        m_ref[...] = jnp.full(m_ref.shape, NEG, F32)
        l_ref[...] = jnp.zeros(l_ref.shape, F32)
        a_ref[...] = jnp.zeros(a_ref.shape, F32)

    q = q_ref[...].astype(BF16)
    qg = [_stack_heads(q, g, per, dh) for g in range(nkv)]
    for r in range(G):
        t_i, off = (r * page) // tw, (r * page) % tw
        bias = jnp.concatenate([bias_ref[t_i, :, off:off + page]] * per, axis=0)
        for g in range(nkv):
            kb = kp[r][0, :, g * dh:(g + 1) * dh].astype(BF16)
            vb = vp[r][0, :, g * dh:(g + 1) * dh].astype(BF16)
            _gqa_update(g, _dot_nt(qg[g], kb) + bias, vb, m_ref, l_ref, a_ref)

    @pl.when(p == pl.num_programs(1) - 1)
    def _():
        kn = _pad_rows(kn_ref[...], page).astype(BF16)
        vn = _pad_rows(vn_ref[...], page).astype(BF16)
        bias = jnp.concatenate([bnew_ref[0, :, 0:page]] * per, axis=0)
        for g in range(nkv):
            _gqa_update(g, _dot_nt(qg[g], kn[:, g * dh:(g + 1) * dh]) + bias, vn[:, g * dh:(g + 1) * dh],
                        m_ref, l_ref, a_ref)
        o_ref[...] = _unstack_heads(a_ref, l_ref, nkv, per, td)


def _dsa_dec_attn(pt, q, kn, vn, bias_tiles, pool_k, pool_v, bd, td, nkv, G):
    m, w = q.shape
    page = pool_k.shape[1]
    kvw = pool_k.shape[2]
    dh = kvw // nkv
    per = w // dh // nkv
    npg = pt.shape[1] // G
    nt, _, tw = bias_tiles.shape
    tiles_per_step = G * page // tw
    seq = lambda width: pl.BlockSpec((td, width), lambda s, p, pt_: (s, 0))

    def page_spec(r):
        return pl.BlockSpec((1, page, kvw), lambda s, p, pt_: (pt_[s, p * G + r], 0, 0))

    grid_spec = pltpu.PrefetchScalarGridSpec(
        num_scalar_prefetch=1,
        grid=(bd, npg),
        in_specs=[seq(w), seq(kvw), seq(kvw),
                  pl.BlockSpec((tiles_per_step, td, tw), lambda s, p, pt_: (p, s, 0)),
                  pl.BlockSpec((1, td, tw), lambda s, p, pt_: (nt - 1, s, 0))]
                 + [page_spec(r) for r in range(G)] * 2,
        out_specs=seq(w),
        scratch_shapes=[pltpu.VMEM((nkv, per * td, LANES), F32), pltpu.VMEM((nkv, per * td, LANES), F32),
                        pltpu.VMEM((nkv, per * td, dh), F32)],
    )
    return pl.pallas_call(
        functools.partial(_dsa_dec_attn_kernel, G=G, td=td, nkv=nkv, per=per, dh=dh, page=page),
        grid_spec=grid_spec,
        out_shape=jax.ShapeDtypeStruct((m, w), F32),
        compiler_params=_cp(("arbitrary", "arbitrary")),
        name="dsa_dec_attn",
    )(pt, q, kn, vn, bias_tiles, bias_tiles, *([pool_k] * G), *([pool_v] * G))


def kernel(x_prompt, x_sample, cache_diff_k, cache_diff_v, state_hgrn, cache_dsa_k, cache_dsa_v, cache_dsa_ik,
           cache_mem_k, cache_mem_v, state_ffn_conv, page_table, mem_prompt, norm_mix, w_in_ab, diff_lq1, diff_lk1,
           diff_lq2, diff_lk2, diff_subln, hgrn_lb_logits, hgrn_onorm, w_out_ab, w_in_c, idx_k_norm, w_out_c,
           norm_x, norm_mem, w_xq, w_xk, w_xv, w_xo, norm_ffn, w_ffn_in, conv_ffn_w, conv_ffn_b, w_ffn_out,
           norm_final):
    b, t, d = x_prompt.shape
    bd, td, _ = x_sample.shape
    depth = norm_mix.shape[0]
    n_ab = w_in_ab.shape[0]
    ha, dva = cache_diff_v.shape[3], cache_diff_v.shape[4]
    dqa = cache_diff_k.shape[4] // 2
    hb, dkb = state_hgrn.shape[2], state_hgrn.shape[3]
    kvc, dhc = cache_dsa_k.shape[3], cache_dsa_k.shape[4]
    hc = d // dhc
    di = cache_dsa_ik.shape[3]
    hi = w_in_c.shape[2] - (hc + 2 * kvc) * dhc - di
    hi = hi // (di + 1)
    hx = cache_mem_k.shape[3]
    n_mem = mem_prompt.shape[1]
    dff = w_ffn_out.shape[1]
    page = cache_diff_k.shape[2]
    past = page_table.shape[1] * page
    mp, ms = b * t, bd * td
    G = 8

    pos_p = jnp.tile(jnp.arange(t), b)
    pos_s = past + jnp.tile(jnp.arange(td), bd)
    tab64_p, half64 = _rope_tables(pos_p, dqa)
    tab64_s, _ = _rope_tables(pos_s, dqa)
    tab128_p, half128 = _rope_tables(pos_p, dhc)
    tab128_s, _ = _rope_tables(pos_s, dhc)

    hgrn_lb = jnp.cumsum(jax.nn.softmax(hgrn_lb_logits.astype(F32), axis=0), axis=0)[:n_ab]
    xp = x_prompt.reshape(mp, d)
    xs = x_sample.reshape(ms, d)
    gfin = norm_final

    mkf, mvf, mkb, mvb = _memproj(mem_prompt.reshape(b * n_mem, d), norm_mem, w_xk.astype(BF16), w_xv.astype(BF16), 256)

    new_p = {k: [] for k in ('diff_k', 'diff_v', 'hgrn', 'dsa_k', 'dsa_v', 'dsa_ik', 'conv')}
    new_s = {k: [] for k in new_p}

    for l in range(depth):
        j = l // 2
        wq_b, wo_b = w_xq[l].astype(BF16), w_xo[l].astype(BF16)
        if l % 2 == 0:
            w_in = w_in_ab[j].astype(BF16)
            w_out = w_out_ab[j].astype(BF16)
            wa, wb = w_out[:ha * dva], w_out[ha * dva:]
            lam_init = 0.8 - 0.6 * math.exp(-0.3 * l)
            lam = (jnp.exp(jnp.sum(diff_lq1[j].astype(F32) * diff_lk1[j].astype(F32)))
                   - jnp.exp(jnp.sum(diff_lq2[j].astype(F32) * diff_lk2[j].astype(F32))) + lam_init)
            q, kf, vf, kb, vb, qh, kk, lf, ib, gb = _inproj_ab(
                xp, norm_mix[l], w_in, hgrn_lb[j], tab64_p, half64, dqa, dkb, BF16, 256)
            o_a = _diffattn_prompt(lam, q, kb, vb, diff_subln[j], b, t, ha, dqa, 1.0 - lam_init, 256)
            o_b, s_new = _hgrn(qh, kk, ib, lf, gb, hgrn_onorm[j], None, b, t, hb, 512, 128, 16, BF16)
            new_p['diff_k'].append(kf.reshape(b, t, ha, 2 * dqa))
            new_p['diff_v'].append(vf.reshape(b, t, ha, dva))
            new_p['hgrn'].append(s_new)
            parts_p = [(o_a, wa), (o_b, wb)]
            q, kf, vf, kb, vb, qh, kk, lf, ib, gb = _inproj_ab(
                xs, norm_mix[l], w_in, hgrn_lb[j], tab64_s, half64, dqa, dkb, F32, ms)
            pool_k = cache_diff_k[j].reshape(-1, page, ha * 2 * dqa)
            pool_v = cache_diff_v[j].reshape(-1, page, ha * dva)
            o_a = _diff_decode(page_table, lam, q, kf, vf, diff_subln[j], pool_k, pool_v, bd, td, ha, dqa,
                               1.0 - lam_init, G)
            c_s = math.gcd(td, 64)
            o_b, s_new = _hgrn(qh, kk, ib, lf, gb, hgrn_onorm[j], state_hgrn[j], bd, td, hb, td, c_s,
                               min(16, c_s), F32)
            new_s['diff_k'].append(kf.reshape(bd, td, ha, 2 * dqa))
            new_s['diff_v'].append(vf.reshape(bd, td, ha, dva))
            new_s['hgrn'].append(s_new)
            parts_s = [(o_a, wa), (o_b, wb)]
        else:
            w_in = w_in_c[j].astype(BF16)
            cw = w_in.shape[1]
            main = (hc + 2 * kvc) * dhc + hi * di
            w_pad = jnp.concatenate([w_in, jnp.zeros((d, main + LANES - cw), BF16)], axis=1)
            ikg_pad = jnp.concatenate([idx_k_norm[j].astype(F32), jnp.zeros((LANES - di,), F32)]).reshape(1, LANES)
            dims = (hc * dhc, kvc * dhc, hi * di, di, hi, dhc)
            w_out = w_out_c[j].astype(BF16)
            q, kf, vf, kb, vb, iq, tail, ik2 = _inproj_c(
                xp, norm_mix[l], w_pad, ikg_pad, tab128_p, tab64_p, half128, half64, dims, BF16, 256)
            topk = min(DSA_TOPK_MAX, t // 4)
            bias = _dsa_index_prompt(iq, tail, ik2, b, t, topk, hi, di, 128, 512)
            o_c = _dsa_attn_prompt(q, kb, vb, bias, b, t, kvc, 128, 512)
            new_p['dsa_k'].append(kf.reshape(b, t, kvc, dhc))
            new_p['dsa_v'].append(vf.reshape(b, t, kvc, dhc))
            new_p['dsa_ik'].append(tail[:, :di].reshape(b, t, di))
            parts_p = [(o_c, w_out)]
            q, kf, vf, kb, vb, iq, tail, ik2 = _inproj_c(
                xs, norm_mix[l], w_pad, ikg_pad, tab128_s, tab64_s, half128, half64, dims, F32, ms)
            tw = 512
            iq_hm = iq.reshape(bd, td, hi, di).transpose(0, 2, 1, 3).reshape(bd, hi * td, di)
            iw_hm = tail[:, di:di + hi].reshape(bd, td, hi).transpose(0, 2, 1).reshape(bd, hi * td, 1)
            ik_new = tail[:, :di]
            sc_past, sc_new = _dsa_dec_scores(page_table, iq_hm, iw_hm, ik_new, cache_dsa_ik[j], bd, td, hi, G, tw)
            sc = jnp.concatenate([sc_past, sc_new], axis=2).reshape(ms, -1)
            nt = sc.shape[1] // tw
            sc_tiles = sc.reshape(ms, nt, tw).transpose(1, 0, 2)
            topk = min(DSA_TOPK_MAX, (past + td) // 4)
            bias_tiles = _dsa_dec_select(sc_tiles, topk, 128 if ms % 128 == 0 else ms)
            pool_k = cache_dsa_k[j].reshape(-1, page, kvc * dhc)
            pool_v = cache_dsa_v[j].reshape(-1, page, kvc * dhc)
            o_c = _dsa_dec_attn(page_table, q, kf, vf, bias_tiles, pool_k, pool_v, bd, td, kvc, G)
            new_s['dsa_k'].append(kf.reshape(bd, td, kvc, dhc))
            new_s['dsa_v'].append(vf.reshape(bd, td, kvc, dhc))
            new_s['dsa_ik'].append(ik_new.reshape(bd, td, di))
            parts_s = [(o_c, w_out)]

        last = l == depth - 1
        wi_b, wo2_b = w_ffn_in[l].astype(BF16), w_ffn_out[l].astype(BF16)
        xp = _cross_prompt(xp, parts_p, norm_x[l], wq_b, wo_b, mkb[l].reshape(b, n_mem, d), mvb[l].reshape(b, n_mem, d),
                           hx, t, 256)
        xp, conv_p = _ffn(xp, norm_ffn[l], wi_b, conv_ffn_w[l], conv_ffn_b[l], wo2_b, gfin, last, t, 256)
        new_p['conv'].append(conv_p)
        xs = _cross_sample(xs, parts_s, norm_x[l], wq_b, wo_b, cache_mem_k[l].reshape(bd, n_mem, d),
                           cache_mem_v[l].reshape(bd, n_mem, d), hx, bd, td)
        st = state_ffn_conv[l]
        zeros = jnp.zeros((bd, td - 1, dff), F32)
        fill1 = jnp.concatenate([st[:, 1:2], zeros], axis=1).reshape(ms, dff)
        fill2 = jnp.concatenate([st, zeros[:, 1:]], axis=1).reshape(ms, dff)
        xs, gate_s = _ffn(xs, norm_ffn[l], wi_b, conv_ffn_w[l], conv_ffn_b[l], wo2_b, gfin, last, td, ms,
                          fills=(fill1, fill2))
        new_s['conv'].append(gate_s.reshape(bd, td, dff)[:, td - 2:])

    sp = {k: jnp.stack(v, axis=0) for k, v in new_p.items()}
    ss = {k: jnp.stack(v, axis=0) for k, v in new_s.items()}
    mk_p = mkf.reshape(depth, b, n_mem, hx, d // hx)
    mv_p = mvf.reshape(depth, b, n_mem, hx, d // hx)
    return (xp.reshape(b, t, d), xs.reshape(bd, td, d),
            sp['diff_k'], sp['diff_v'], sp['hgrn'], sp['dsa_k'], sp['dsa_v'], sp['dsa_ik'],
            mk_p, mv_p, sp['conv'],
            ss['diff_k'], ss['diff_v'], ss['hgrn'], ss['dsa_k'], ss['dsa_v'], ss['dsa_ik'], ss['conv'])
```

```python
import functools
import math

import jax
import jax.numpy as jnp
import numpy as np
from jax import lax
from jax.experimental import pallas as pl
from jax.experimental.pallas import tpu as pltpu

F32 = jnp.float32
BF16 = jnp.bfloat16
I32 = jnp.int32

EPS = 1e-6
ROPE_THETA = 500000.0
ROT_DIV = 4
DSA_TOPK_MAX = 256
NEG = -1e30
INT_MIN = -2147483648
LANES = 128
VMEM_LIMIT = 56 * 1024 * 1024

NT = (((1,), (1,)), ((), ()))


def _cp(sem):
    return pltpu.CompilerParams(dimension_semantics=sem, vmem_limit_bytes=VMEM_LIMIT)


def _dot(a, b):
    return jnp.dot(a, b, preferred_element_type=F32)


def _dot_nt(a, b):
    return lax.dot_general(a, b, NT, preferred_element_type=F32)


def _rms(x, g):
    ms = jnp.mean(x * x, axis=-1, keepdims=True)
    return x * lax.rsqrt(ms + EPS) * g


def _silu(x):
    return x * jax.nn.sigmoid(x)


def _tile_lanes(t, n):
    return t if n == 1 else jnp.concatenate([t] * n, axis=1)


def _rope(z, c, s1, s2, half):
    w = z.shape[1]
    n = w // LANES
    return (z * _tile_lanes(c, n) + pltpu.roll(z, half, 1) * _tile_lanes(s1, n)
            + pltpu.roll(z, w - half, 1) * _tile_lanes(s2, n))


def _rope_tables(pos, head_dim):
    half = head_dim // ROT_DIV // 2
    inv = jnp.exp(jnp.arange(half, dtype=F32) * (-math.log(ROPE_THETA) / half))
    ang = pos.astype(F32)[:, None] * inv[None, :]
    cos, sin = jnp.cos(ang), jnp.sin(ang)
    m = pos.shape[0]
    rest = head_dim - 2 * half
    c = jnp.concatenate([cos, cos, jnp.ones((m, rest), F32)], axis=1)
    s1 = jnp.concatenate([jnp.zeros((m, half), F32), sin, jnp.zeros((m, rest), F32)], axis=1)
    s2 = jnp.concatenate([-sin, jnp.zeros((m, half + rest), F32)], axis=1)
    rep = LANES // head_dim
    return tuple(jnp.tile(t, (1, rep)) for t in (c, s1, s2)), half


def _const_spec(shape):
    nd = len(shape)
    return pl.BlockSpec(shape, lambda *_: (0,) * nd)


def _inproj_ab_kernel(x_ref, g_ref, w_ref, lb_ref, c_ref, s1_ref, s2_ref,
                      q_ref, kf_ref, vf_ref, kb_ref, vb_ref, qh_ref, kk_ref, lf_ref, ib_ref, gb_ref,
                      *, half, qscale, hscale):
    h = _rms(x_ref[...], g_ref[...]).astype(BF16)
    sw = q_ref.shape[1]

    def seg(i):
        return _dot(h, w_ref[:, i * sw:(i + 1) * sw])

    c, s1, s2 = c_ref[...], s1_ref[...], s2_ref[...]
    q_ref[...] = (_rope(seg(0), c, s1, s2, half) * qscale).astype(q_ref.dtype)
    ka = _rope(seg(1), c, s1, s2, half)
    kf_ref[...] = ka
    kb_ref[...] = ka.astype(BF16)
    va = seg(2)
    vf_ref[...] = va
    vb_ref[...] = va.astype(BF16)
    qh_ref[...] = _silu(seg(3)) * hscale
    lb = lb_ref[...]
    fg = lb + (1.0 - lb) * jax.nn.sigmoid(seg(4))
    kk_ref[...] = 1.0 - fg
    lf_ref[...] = jnp.log(fg)
    ib_ref[...] = seg(5)
    gb_ref[...] = _silu(seg(6))


def _inproj_ab(x, g, w, lb, tabs, half, dqa, dkb, qdtype, tm):
    m, d = x.shape
    sw = w.shape[1] // 7
    row = lambda width: pl.BlockSpec((tm, width), lambda i: (i, 0))
    outs = [(sw, qdtype), (sw, F32), (sw, F32), (sw, BF16), (sw, BF16)] + [(sw, F32)] * 5
    return pl.pallas_call(
        functools.partial(_inproj_ab_kernel, half=half, qscale=dqa ** -0.5, hscale=dkb ** -0.5),
        grid=(m // tm,),
        in_specs=[row(d), _const_spec((1, d)), _const_spec(w.shape), _const_spec((1, sw)),
                  row(LANES), row(LANES), row(LANES)],
        out_specs=[row(wd) for wd, _ in outs],
        out_shape=[jax.ShapeDtypeStruct((m, wd), dt) for wd, dt in outs],
        compiler_params=_cp(("arbitrary",)),
        name="inproj_ab",
    )(x, g.reshape(1, d), w, lb.reshape(1, sw), *tabs)


def _inproj_c_kernel(x_ref, g_ref, w_ref, ikg_ref, c128_ref, s1128_ref, s2128_ref, c64_ref, s164_ref, s264_ref,
                     q_ref, kf_ref, vf_ref, kb_ref, vb_ref, iq_ref, tail_ref, ik2_ref,
                     *, nq, nkv, niq, di, half128, half64, qscale, iwscale):
    h = _rms(x_ref[...], g_ref[...]).astype(BF16)
    c1, a1, b1 = c128_ref[...], s1128_ref[...], s2128_ref[...]
    c6, a6, b6 = c64_ref[...], s164_ref[...], s264_ref[...]
    o = 0
    q = _dot(h, w_ref[:, o:o + nq]); o += nq
    q_ref[...] = (_rope(q, c1, a1, b1, half128) * qscale).astype(q_ref.dtype)
    k = _rope(_dot(h, w_ref[:, o:o + nkv]), c1, a1, b1, half128); o += nkv
    kf_ref[...] = k
    kb_ref[...] = k.astype(BF16)
    v = _dot(h, w_ref[:, o:o + nkv]); o += nkv
    vf_ref[...] = v
    vb_ref[...] = v.astype(BF16)
    iq = _rope(_dot(h, w_ref[:, o:o + niq]), c6, a6, b6, half64); o += niq
    iq_ref[...] = iq.astype(iq_ref.dtype)
    t = _dot(h, w_ref[:, o:o + LANES])
    lane = lax.broadcasted_iota(I32, t.shape, 1)
    is_k = lane < di
    tk = jnp.where(is_k, t, 0.0)
    ms = jnp.sum(tk * tk, axis=-1, keepdims=True) * (1.0 / di)
    ikn = tk * lax.rsqrt(ms + EPS) * ikg_ref[...]
    ikr = _rope(ikn, c6, a6, b6, half64)
    ikr = jnp.where(is_k, ikr, 0.0)
    tail_ref[...] = jnp.where(is_k, ikr, t * iwscale)
    ik2_ref[...] = (ikr + pltpu.roll(ikr, di, 1)).astype(BF16)


def _inproj_c(x, g, w_pad, ikg_pad, tabs128, tabs64, half128, half64, dims, qdtype, tm):
    m, d = x.shape
    nq, nkv, niq, di, hi, dhc = dims
    row = lambda width: pl.BlockSpec((tm, width), lambda i: (i, 0))
    outs = [(nq, qdtype), (nkv, F32), (nkv, F32), (nkv, BF16), (nkv, BF16), (niq, qdtype), (LANES, F32), (LANES, BF16)]
    return pl.pallas_call(
        functools.partial(_inproj_c_kernel, nq=nq, nkv=nkv, niq=niq, di=di, half128=half128, half64=half64,
                          qscale=dhc ** -0.5, iwscale=(hi * di) ** -0.5),
        grid=(m // tm,),
        in_specs=[row(d), _const_spec((1, d)), _const_spec(w_pad.shape), _const_spec((1, LANES))]
                 + [row(LANES)] * 6,
        out_specs=[row(wd) for wd, _ in outs],
        out_shape=[jax.ShapeDtypeStruct((m, wd), dt) for wd, dt in outs],
        compiler_params=_cp(("arbitrary",)),
        name="inproj_c",
    )(x, g.reshape(1, d), w_pad, ikg_pad, *tabs128, *tabs64)


def _memproj_kernel(x_ref, g_ref, wk_ref, wv_ref, kf_ref, vf_ref, kb_ref, vb_ref):
    h = _rms(x_ref[...], g_ref[0]).astype(BF16)
    k = _dot(h, wk_ref[0])
    v = _dot(h, wv_ref[0])
    kf_ref[0] = k
    vf_ref[0] = v
    kb_ref[0] = k.astype(BF16)
    vb_ref[0] = v.astype(BF16)


def _memproj(x, g, wk, wv, tm):
    m, d = x.shape
    depth, _, n = wk.shape
    wspec = pl.BlockSpec((1, d, n), lambda l, i: (l, 0, 0))
    ospec = pl.BlockSpec((1, tm, n), lambda l, i: (l, i, 0))
    return pl.pallas_call(
        _memproj_kernel,
        grid=(depth, m // tm),
        in_specs=[pl.BlockSpec((tm, d), lambda l, i: (i, 0)), pl.BlockSpec((1, 1, d), lambda l, i: (l, 0, 0)),
                  wspec, wspec],
        out_specs=[ospec] * 4,
        out_shape=[jax.ShapeDtypeStruct((depth, m, n), dt) for dt in (F32, F32, BF16, BF16)],
        compiler_params=_cp(("arbitrary", "arbitrary")),
        name="memproj",
    )(x, g.reshape(depth, 1, d), wk, wv)


def _softmax_step(s, vb, m, l, a):
    m_new = jnp.maximum(m, jnp.max(s, axis=-1, keepdims=True))
    alpha = jnp.exp(m - m_new)
    p = jnp.exp(s - m_new)
    l = alpha * l + jnp.sum(p, axis=-1, keepdims=True)
    a = alpha * a + _dot(p.astype(BF16), vb)
    return m_new, l, a


def _diff_finish(a1, l1, a2, l2, lam, g, out_scale):
    o = a1 / l1 - lam * (a2 / l2)
    return _rms(o, g) * out_scale


def _diffattn_kernel(lam_ref, q_ref, k_ref, v_ref, g_ref, o_ref, *, tq, dq, out_scale):
    i = pl.program_id(2)
    q = q_ref[...]
    lane = lax.broadcasted_iota(I32, q.shape, 1)
    zero = jnp.zeros_like(q)
    q1 = jnp.where(lane < dq, q, zero)
    q2 = jnp.where(lane >= dq, q, zero)
    dv = v_ref.shape[1]
    causal = (lax.broadcasted_iota(I32, (tq, tq), 0) >= lax.broadcasted_iota(I32, (tq, tq), 1))
    causal_bias = jnp.where(causal, 0.0, NEG)

    def scores(j):
        kb = k_ref[pl.ds(pl.multiple_of(j * tq, tq), tq), :]
        bias = causal_bias * (j == i).astype(F32)
        return _dot_nt(q1, kb) + bias, _dot_nt(q2, kb) + bias

    def attend(j, s1, s2, state):
        vb = v_ref[pl.ds(pl.multiple_of(j * tq, tq), tq), :]
        m1, l1, a1, m2, l2, a2 = state
        return _softmax_step(s1, vb, m1, l1, a1) + _softmax_step(s2, vb, m2, l2, a2)

    def init():
        return (jnp.full((tq, 1), NEG, F32), jnp.zeros((tq, 1), F32), jnp.zeros((tq, dv), F32))

    def body(j, carry):
        s1, s2, state = carry
        n1, n2 = scores(j + 1)
        return n1, n2, attend(j, s1, s2, state)

    s1, s2, state = lax.fori_loop(0, i, body, scores(0) + (init() + init(),))
    m1, l1, a1, m2, l2, a2 = attend(i, s1, s2, state)
    o_ref[...] = _diff_finish(a1, l1, a2, l2, lam_ref[0], g_ref[...], out_scale).astype(o_ref.dtype)


def _diffattn_prompt(lam, q, kb, vb, g, b, t, nh, dq, out_scale, tq):
    m, w = q.shape
    dv = w // nh
    nq = t // tq
    smem = pl.BlockSpec(memory_space=pltpu.SMEM)
    return pl.pallas_call(
        functools.partial(_diffattn_kernel, tq=tq, dq=dq, out_scale=out_scale),
        grid=(b, nh, nq),
        in_specs=[smem,
                  pl.BlockSpec((tq, dv), lambda bb, h, i: (bb * nq + i, h)),
                  pl.BlockSpec((t, dv), lambda bb, h, i: (bb, h)),
                  pl.BlockSpec((t, dv), lambda bb, h, i: (bb, h)),
                  _const_spec((1, dv))],
        out_specs=pl.BlockSpec((tq, dv), lambda bb, h, i: (bb * nq + i, h)),
        out_shape=jax.ShapeDtypeStruct((m, w), BF16),
        compiler_params=_cp(("arbitrary",) * 3),
        name="diffattn_prompt",
    )(lam.reshape(1), q, kb, vb, g.reshape(1, dv))


def _pad_rows(x, rows):
    return jnp.concatenate([x, jnp.zeros((rows - x.shape[0], x.shape[1]), x.dtype)], axis=0)


def _diff_decode_kernel(pt_ref, lam_ref, q_ref, kn_ref, vn_ref, g_ref, *rest, G, nh, dq, out_scale, page):
    kp, vp = rest[:G], rest[G:2 * G]
    o_ref, m_ref, l_ref, a_ref = rest[2 * G:]
    p = pl.program_id(1)
    td = q_ref.shape[0]
    dv = q_ref.shape[1] // nh

    @pl.when(p == 0)
    def _():
        m_ref[...] = jnp.full(m_ref.shape, NEG, F32)
        l_ref[...] = jnp.zeros(l_ref.shape, F32)
        a_ref[...] = jnp.zeros(a_ref.shape, F32)

    q = q_ref[...].astype(BF16)
    lane = lax.broadcasted_iota(I32, (td, dv), 1)

    def q_pair(h):
        qh = q[:, h * dv:(h + 1) * dv]
        zero = jnp.zeros_like(qh)
        return jnp.concatenate([jnp.where(lane < dq, qh, zero), jnp.where(lane >= dq, qh, zero)], axis=0)

    def update(h, s, vlist):
        m_old = m_ref[h]
        m_new = jnp.maximum(m_old, jnp.max(s, axis=-1, keepdims=True))
        alpha = jnp.exp(m_old - m_new)
        pm = jnp.exp(s - m_new[:, :1]).astype(BF16)
        l_ref[h] = alpha * l_ref[h] + jnp.sum(pm.astype(F32), axis=-1, keepdims=True)
        pv = jnp.zeros((2 * td, dv), F32)
        for r, vb in enumerate(vlist):
            pv = pv + _dot(pm[:, r * page:(r + 1) * page], vb)
        a_ref[h] = alpha * a_ref[h] + pv
        m_ref[h] = m_new

    for h in range(nh):
        q2 = q_pair(h)
        s = jnp.concatenate([_dot_nt(q2, kp[r][0, :, h * dv:(h + 1) * dv].astype(BF16)) for r in range(G)], axis=1)
        update(h, s, [vp[r][0, :, h * dv:(h + 1) * dv].astype(BF16) for r in range(G)])

    @pl.when(p == pl.num_programs(1) - 1)
    def _():
        row = lax.broadcasted_iota(I32, (2 * td, page), 0)
        col = lax.broadcasted_iota(I32, (2 * td, page), 1)
        tt = jnp.where(row >= td, row - td, row)
        valid = col <= tt
        kn = _pad_rows(kn_ref[...], page).astype(BF16)
        vn = _pad_rows(vn_ref[...], page).astype(BF16)
        outs = []
        for h in range(nh):
            s = jnp.where(valid, _dot_nt(q_pair(h), kn[:, h * dv:(h + 1) * dv]), NEG)
            update(h, s, [vn[:, h * dv:(h + 1) * dv]])
            a, l = a_ref[h], l_ref[h]
            outs.append(_diff_finish(a[:td], l[:td], a[td:], l[td:], lam_ref[0], g_ref[...], out_scale))
        o_ref[...] = jnp.concatenate(outs, axis=1)


def _diff_decode(pt, lam, q, kn, vn, g, pool_k, pool_v, base, bd, td, nh, dq, out_scale, G):
    m, w = q.shape
    dv = w // nh
    npg = pt.shape[1] // G
    page = pool_k.shape[1]
    seq = pl.BlockSpec((td, w), lambda s, p, pt_: (s, 0))

    def page_spec(r):
        return pl.BlockSpec((1, page, w), lambda s, p, pt_: (base + pt_[s, p * G + r], 0, 0))

    grid_spec = pltpu.PrefetchScalarGridSpec(
        num_scalar_prefetch=1,
        grid=(bd, npg),
        in_specs=[pl.BlockSpec(memory_space=pltpu.SMEM), seq, seq, seq,
                  pl.BlockSpec((1, dv), lambda s, p, pt_: (0, 0))]
                 + [page_spec(r) for r in range(G)] * 2,
        out_specs=seq,
        scratch_shapes=[pltpu.VMEM((nh, 2 * td, LANES), F32), pltpu.VMEM((nh, 2 * td, LANES), F32),
                        pltpu.VMEM((nh, 2 * td, dv), F32)],
    )
    return pl.pallas_call(
        functools.partial(_diff_decode_kernel, G=G, nh=nh, dq=dq, out_scale=out_scale, page=page),
        grid_spec=grid_spec,
        out_shape=jax.ShapeDtypeStruct((m, w), F32),
        compiler_params=_cp(("arbitrary", "arbitrary")),
        name="diffattn_decode",
    )(pt, lam.reshape(1), q, kn, vn, g.reshape(1, dv), *([pool_k] * G), *([pool_v] * G))


def _cumsum_rows(x):
    n = x.shape[0]
    row = lax.broadcasted_iota(I32, x.shape, 0)
    s = 1
    while s < n:
        x = x + jnp.where(row >= s, pltpu.roll(x, s, 0), 0.0)
        s *= 2
    return x


def _hgrn_chunk(q, k, v, lf, st, sub):
    c = q.shape[0]
    b = _cumsum_rows(lf)
    o = _dot_nt((q * jnp.exp(b)).astype(BF16), st.astype(BF16))
    row_c = lax.broadcasted_iota(I32, (c, q.shape[1]), 0)
    row_s = lax.broadcasted_iota(I32, (sub, q.shape[1]), 0)
    vb = v.astype(BF16)
    parts = []
    for i in range(c // sub):
        lo = i * sub
        qi, ki, vi, bi = (a[lo:lo + sub] for a in (q, k, v, b))
        acc = jnp.zeros((sub, v.shape[1]), F32)
        for s in range(sub):
            d = jnp.where(row_s >= s, bi - bi[s:s + 1], -jnp.inf)
            a = jnp.sum(jnp.exp(d) * qi * ki[s:s + 1], axis=-1, keepdims=True)
            acc = acc + a * vi[s:s + 1]
        if i > 0:
            r = b[lo - 1:lo]
            qs = (qi * jnp.exp(bi - r)).astype(BF16)
            ks = (k * jnp.exp(jnp.where(row_c < lo, r - b, -jnp.inf))).astype(BF16)
            acc = acc + _dot(_dot_nt(qs, ks).astype(BF16), vb)
        parts.append(acc)
    o = o + (parts[0] if len(parts) == 1 else jnp.concatenate(parts, axis=0))
    bl = b[c - 1:c]
    kd = k * jnp.exp(bl - b)
    if c % LANES:
        pad = -c % LANES
        v, kd = _pad_rows(v, c + pad), _pad_rows(kd, c + pad)
    st = st * jnp.exp(bl) + _dot(v.T.astype(BF16), kd.astype(BF16))
    return o, st


def _hgrn_kernel(*refs, c, sub, has_s0):
    if has_s0:
        q_ref, k_ref, v_ref, lf_ref, gb_ref, on_ref, s0_ref, o_ref, so_ref, st_ref = refs
    else:
        q_ref, k_ref, v_ref, lf_ref, gb_ref, on_ref, o_ref, so_ref, st_ref = refs
    i = pl.program_id(2)

    @pl.when(i == 0)
    def _():
        st_ref[...] = s0_ref[0, 0].T if has_s0 else jnp.zeros(st_ref.shape, F32)

    def body(ci, carry):
        r0 = pl.multiple_of(ci * c, c)
        sl = pl.ds(r0, c)
        o, st = _hgrn_chunk(q_ref[sl, :], k_ref[sl, :], v_ref[sl, :], lf_ref[sl, :], st_ref[...], sub)
        st_ref[...] = st
        o_ref[sl, :] = (_rms(o, on_ref[...]) * gb_ref[sl, :]).astype(o_ref.dtype)
        return carry

    lax.fori_loop(0, q_ref.shape[0] // c, body, 0)

    @pl.when(i == pl.num_programs(2) - 1)
    def _():
        so_ref[0, 0] = st_ref[...].T


def _hgrn(qh, kk, ib, lf, gb, onorm, s0, s0_base, b, t, nh, tc, c, sub, odtype):
    m, w = qh.shape
    dk = w // nh
    nt = t // tc
    blk = pl.BlockSpec((tc, dk), lambda bb, h, i: (bb * nt + i, h))
    st_spec = pl.BlockSpec((1, 1, dk, dk), lambda bb, h, i: (bb, h, 0, 0))
    ins = [qh, kk, ib, lf, gb, onorm.reshape(1, dk)]
    in_specs = [blk] * 5 + [_const_spec((1, dk))]
    if s0 is not None:
        ins.append(s0)
        in_specs.append(pl.BlockSpec((1, 1, dk, dk), lambda bb, h, i: (s0_base + bb, h, 0, 0)))
    return pl.pallas_call(
        functools.partial(_hgrn_kernel, c=c, sub=sub, has_s0=s0 is not None),
        grid=(b, nh, nt),
        in_specs=in_specs,
        out_specs=[blk, st_spec],
        out_shape=[jax.ShapeDtypeStruct((m, w), odtype), jax.ShapeDtypeStruct((b, nh, dk, dk), F32)],
        scratch_shapes=[pltpu.VMEM((dk, dk), F32)],
        compiler_params=_cp(("arbitrary",) * 3),
        name="hgrn",
    )(*ins)


def _mixer_out(x, parts):
    for a_ref, w_ref in parts:
        x = x + _dot(a_ref[...].astype(BF16), w_ref[...])
    return x


def _mem_attend(q, mk, mv, nh):
    dh = q.shape[1] // nh
    outs = []
    for h in range(nh):
        sl = slice(h * dh, (h + 1) * dh)
        s = _dot_nt(q[:, sl], mk[:, sl])
        p = jnp.exp(s - jnp.max(s, axis=-1, keepdims=True))
        p = p / jnp.sum(p, axis=-1, keepdims=True)
        outs.append(_dot(p.astype(BF16), mv[:, sl]))
    return jnp.concatenate(outs, axis=1).astype(BF16)


def _cross_prompt_kernel(*refs, n_parts, nh, qscale):
    x_ref = refs[0]
    parts = [(refs[1 + 2 * j], refs[2 + 2 * j]) for j in range(n_parts)]
    g_ref, wq_ref, wo_ref, mk_ref, mv_ref, o_ref = refs[1 + 2 * n_parts:]
    x1 = _mixer_out(x_ref[...], parts)
    h = _rms(x1, g_ref[...]).astype(BF16)
    q = (_dot(h, wq_ref[...]) * qscale).astype(BF16)
    o = _mem_attend(q, mk_ref[0], mv_ref[0], nh)
    o_ref[...] = x1 + _dot(o, wo_ref[...])


def _cross_prompt(x, parts, g, wq, wo, mk, mv, nh, t, tm):
    m, d = x.shape
    n_mem = mk.shape[1]
    per = t // tm
    row = lambda width: pl.BlockSpec((tm, width), lambda i: (i, 0))
    mem = pl.BlockSpec((1, n_mem, d), lambda i: (i // per, 0, 0))
    ins, in_specs = [x], [row(d)]
    for a, w in parts:
        ins += [a, w]
        in_specs += [row(a.shape[1]), _const_spec(w.shape)]
    ins += [g.reshape(1, d), wq, wo, mk, mv]
    in_specs += [_const_spec((1, d)), _const_spec(wq.shape), _const_spec(wo.shape), mem, mem]
    return pl.pallas_call(
        functools.partial(_cross_prompt_kernel, n_parts=len(parts), nh=nh, qscale=(d // nh) ** -0.5),
        grid=(m // tm,),
        in_specs=in_specs,
        out_specs=row(d),
        out_shape=jax.ShapeDtypeStruct((m, d), F32),
        compiler_params=_cp(("arbitrary",)),
        name="cross_prompt",
    )(*ins)


def _cross_sample_kernel(*refs, n_parts, nh, qscale, td):
    x_ref = refs[0]
    parts = [(refs[1 + 2 * j], refs[2 + 2 * j]) for j in range(n_parts)]
    g_ref, wq_ref, wo_ref, mk_ref, mv_ref, o_ref, x1_ref, q_ref, a_ref = refs[1 + 2 * n_parts:]
    s = pl.program_id(0)

    @pl.when(s == 0)
    def _():
        x1 = _mixer_out(x_ref[...], parts)
        x1_ref[...] = x1
        q_ref[...] = _dot(_rms(x1, g_ref[...]).astype(BF16), wq_ref[...]) * qscale

    sl = pl.ds(pl.multiple_of(s * td, td), td)
    a_ref[sl, :] = _mem_attend(q_ref[sl, :].astype(BF16), mk_ref[0].astype(BF16), mv_ref[0].astype(BF16),
                               nh).astype(F32)

    @pl.when(s == pl.num_programs(0) - 1)
    def _():
        o_ref[...] = x1_ref[...] + _dot(a_ref[...].astype(BF16), wo_ref[...])


def _cross_sample(x, parts, g, wq, wo, mk, mv, base, nh, bd, td):
    m, d = x.shape
    n_mem = mk.shape[1]
    mem = pl.BlockSpec((1, n_mem, d), lambda s: (base + s, 0, 0))
    ins, in_specs = [x], [_const_spec((m, d))]
    for a, w in parts:
        ins += [a, w]
        in_specs += [_const_spec(a.shape), _const_spec(w.shape)]
    ins += [g.reshape(1, d), wq, wo, mk, mv]
    in_specs += [_const_spec((1, d)), _const_spec(wq.shape), _const_spec(wo.shape), mem, mem]
    return pl.pallas_call(
        functools.partial(_cross_sample_kernel, n_parts=len(parts), nh=nh, qscale=(d // nh) ** -0.5, td=td),
        grid=(bd,),
        in_specs=in_specs,
        out_specs=_const_spec((m, d)),
        out_shape=jax.ShapeDtypeStruct((m, d), F32),
        scratch_shapes=[pltpu.VMEM((m, d), F32)] * 3,
        compiler_params=_cp(("arbitrary",)),
        name="cross_sample",
    )(*ins)


def _ffn_kernel(*refs, carry_mode, seq_rows, blocks_per_seq, nchunk, cw, final_norm):
    if carry_mode:
        (x_ref, g_ref, wi_ref, cwt_ref, cb_ref, wo_ref, gf_ref, o_ref, st_ref, prev_ref) = refs
    else:
        (x_ref, g_ref, wi_ref, cwt_ref, cb_ref, wo_ref, gf_ref, f1_ref, f2_ref, o_ref, gate_ref) = refs
    x = x_ref[...]
    tm = x.shape[0]
    dff = cwt_ref.shape[1]
    h = _rms(x, g_ref[...]).astype(BF16)
    row = lax.broadcasted_iota(I32, (tm, cw), 0)
    if carry_mode:
        @pl.when(pl.program_id(0) % blocks_per_seq == 0)
        def _():
            prev_ref[...] = jnp.zeros(prev_ref.shape, F32)
    else:
        rseq = row % seq_rows
    acc = jnp.zeros(x.shape, F32)
    for ci in range(nchunk):
        c0 = ci * cw
        cs = slice(c0, c0 + cw)
        gt = _dot(h, wi_ref[:, cs])
        u = _dot(h, wi_ref[:, dff + c0:dff + c0 + cw])
        if carry_mode:
            p0, p1 = prev_ref[0:1, cs], prev_ref[1:2, cs]
            g1 = jnp.where(row == 0, p1, pltpu.roll(gt, 1, 0))
            g2 = jnp.where(row == 0, p0, jnp.where(row == 1, p1, pltpu.roll(gt, 2, 0)))
            last = gt[tm - 2:tm]
            prev_ref[:, cs] = last
            st_ref[0, :, cs] = last
        else:
            g1 = jnp.where(rseq >= 1, pltpu.roll(gt, 1, 0), f1_ref[:, cs])
            g2 = jnp.where(rseq >= 2, pltpu.roll(gt, 2, 0), f2_ref[:, cs])
            gate_ref[:, cs] = gt
        gc = cb_ref[:, cs] + g2 * cwt_ref[0:1, cs] + g1 * cwt_ref[1:2, cs] + gt * cwt_ref[2:3, cs]
        acc = acc + _dot((_silu(gc) * u).astype(BF16), wo_ref[cs, :])
    y = x + acc
    if final_norm:
        y = _rms(y, gf_ref[...])
    o_ref[...] = y


def _ffn(x, g, wi, cwt, cb, wo, gfinal, final_norm, seq_rows, tm, fills=None):
    m, d = x.shape
    dff = wo.shape[0]
    cw = 256
    carry_mode = fills is None
    nseq = m // seq_rows
    row = lambda width: pl.BlockSpec((tm, width), lambda i: (i, 0))
    ins = [x, g.reshape(1, d), wi, cwt, cb.reshape(1, dff), wo, gfinal.reshape(1, d)]
    in_specs = [row(d), _const_spec((1, d)), _const_spec(wi.shape), _const_spec(cwt.shape),
                _const_spec((1, dff)), _const_spec(wo.shape), _const_spec((1, d))]
    if carry_mode:
        per = seq_rows // tm
        out_specs = [row(d), pl.BlockSpec((1, 2, dff), lambda i: (i // per, 0, 0))]
        out_shape = [jax.ShapeDtypeStruct((m, d), F32), jax.ShapeDtypeStruct((nseq, 2, dff), F32)]
        scratch = [pltpu.VMEM((2, dff), F32)]
    else:
        per = 1
        ins += list(fills)
        in_specs += [row(dff), row(dff)]
        out_specs = [row(d), row(dff)]
        out_shape = [jax.ShapeDtypeStruct((m, d), F32), jax.ShapeDtypeStruct((m, dff), F32)]
        scratch = []
    return pl.pallas_call(
        functools.partial(_ffn_kernel, carry_mode=carry_mode, seq_rows=seq_rows, blocks_per_seq=per,
                          nchunk=dff // cw, cw=cw, final_norm=final_norm),
        grid=(m // tm,),
        in_specs=in_specs,
        out_specs=out_specs,
        out_shape=out_shape,
        scratch_shapes=scratch,
        compiler_params=_cp(("arbitrary",)),
        name="ffn",
    )(*ins)


def _sort_key(x):
    bits = lax.bitcast_convert_type(x, I32)
    key = jnp.where(bits < 0, bits ^ 0x7FFFFFFF, bits)
    return jnp.where(x == 0.0, 0, key)


def _fold_lanes(c):
    parts = [c[:, j * LANES:(j + 1) * LANES] for j in range(c.shape[1] // LANES)]
    while len(parts) > 1:
        parts = [parts[j] + parts[j + 1] for j in range(0, len(parts) - 1, 2)] + (parts[-1:] if len(parts) % 2 else [])
    return parts[0]


def _select_topk(keys_ref, n_tiles, topk, idx_bits):
    _, rows, tw = keys_ref.shape
    lane = lax.broadcasted_iota(I32, (rows, tw), 1)

    def count(ind):
        def body(j, acc):
            return acc + _fold_lanes(ind(keys_ref[j], j))
        acc = lax.fori_loop(0, n_tiles, body, jnp.zeros((rows, LANES), F32))
        return jnp.sum(acc, axis=-1, keepdims=True)

    def thr_step(it, thr):
        trial = thr ^ jnp.left_shift(jnp.int32(1), 31 - it)
        cnt = count(lambda kt, j: jnp.where(kt >= trial, 1.0, 0.0))
        return jnp.where(cnt >= topk, trial, thr)

    thr = lax.fori_loop(0, 32, thr_step, jnp.full((rows, 1), INT_MIN, I32))
    thr = jnp.maximum(thr, INT_MIN + 1)
    need = topk - count(lambda kt, j: jnp.where(kt > thr, 1.0, 0.0))
    ties = count(lambda kt, j: jnp.where(kt == thr, 1.0, 0.0))
    excess = jnp.max(ties - need) > 0.0

    def cut_step(it, cut):
        trial = cut | jnp.left_shift(jnp.int32(1), idx_bits - 1 - it)
        cnt = count(lambda kt, j: jnp.where(kt == thr, jnp.where(lane + j * tw < trial, 1.0, 0.0), 0.0))
        return jnp.where(cnt <= need, trial, cut)

    cut0 = jnp.full((rows, 1), jnp.where(excess, 0, 1 << idx_bits), I32)
    cut = lax.fori_loop(0, jnp.where(excess, idx_bits, 0), cut_step, cut0)
    return thr, cut


def _selection_bias(kt, j, tw, thr, cut):
    lane = lax.broadcasted_iota(I32, kt.shape, 1)
    return jnp.where(kt > thr, 0.0, jnp.where(kt == thr, jnp.where(lane + j * tw < cut, 0.0, NEG), NEG))


def _index_scores(iq, iw, ik2_tile, nhi, di):
    lane = lax.broadcasted_iota(I32, (iq.shape[0], LANES), 1)
    acc = None
    for pr in range(nhi * di // LANES):
        pair = iq[:, pr * LANES:(pr + 1) * LANES]
        zero = jnp.zeros_like(pair)
        for half_i, qm in enumerate((jnp.where(lane < di, pair, zero), jnp.where(lane >= di, pair, zero))):
            hd = 2 * pr + half_i
            term = jnp.maximum(_dot_nt(qm, ik2_tile), 0.0) * iw[:, hd:hd + 1]
            acc = term if acc is None else acc + term
    return acc


def _dsa_index_kernel(iq_ref, tail_ref, ik2_ref, bias_ref, keys_ref, *, tq, tw, topk, nhi, di, idx_bits):
    i = pl.program_id(1)
    nt_all = keys_ref.shape[0]
    n_tiles = (i * tq + tq + tw - 1) // tw
    iq = iq_ref[...]
    iw = tail_ref[:, di:di + nhi]
    qpos = i * tq + lax.broadcasted_iota(I32, (tq, tw), 0)
    lane = lax.broadcasted_iota(I32, (tq, tw), 1)

    def score_tile(j, c):
        r0 = pl.multiple_of(j * tw, tw)
        sc = _index_scores(iq, iw, ik2_ref[pl.ds(r0, tw), :], nhi, di)
        keys_ref[j] = jnp.where(lane + j * tw <= qpos, _sort_key(sc), INT_MIN)
        return c

    lax.fori_loop(0, n_tiles, score_tile, 0)
    thr, cut = _select_topk(keys_ref, n_tiles, topk, idx_bits)

    def write_tile(j, c):
        bias_ref[0, j] = _selection_bias(keys_ref[j], j, tw, thr, cut).astype(bias_ref.dtype)
        return c

    lax.fori_loop(0, n_tiles, write_tile, 0)

    def fill_tile(j, c):
        bias_ref[0, j] = jnp.full((tq, tw), NEG, bias_ref.dtype)
        return c

    lax.fori_loop(n_tiles, nt_all, fill_tile, 0)


def _dsa_index_prompt(iq, tail, ik2, b, t, topk, nhi, di, tq, tw):
    m = iq.shape[0]
    nq = t // tq
    nt = t // tw
    return pl.pallas_call(
        functools.partial(_dsa_index_kernel, tq=tq, tw=tw, topk=topk, nhi=nhi, di=di,
                          idx_bits=max(1, t.bit_length())),
        grid=(b, nq),
        in_specs=[pl.BlockSpec((tq, iq.shape[1]), lambda bb, i: (bb * nq + i, 0)),
                  pl.BlockSpec((tq, LANES), lambda bb, i: (bb * nq + i, 0)),
                  pl.BlockSpec((t, LANES), lambda bb, i: (bb, 0))],
        out_specs=pl.BlockSpec((1, nt, tq, tw), lambda bb, i: (bb * nq + i, 0, 0, 0)),
        out_shape=jax.ShapeDtypeStruct((m // tq, nt, tq, tw), BF16),
        scratch_shapes=[pltpu.VMEM((nt, tq, tw), I32)],
        compiler_params=_cp(("arbitrary", "arbitrary")),
        name="dsa_index",
    )(iq, tail, ik2)


def _gqa_update(g, s, vlist, m_ref, l_ref, a_ref):
    m_old = m_ref[g]
    m_new = jnp.maximum(m_old, jnp.max(s, axis=-1, keepdims=True))
    alpha = jnp.exp(m_old - m_new)
    p = jnp.exp(s - m_new[:, :1])
    l_ref[g] = alpha * l_ref[g] + jnp.sum(p, axis=-1, keepdims=True)
    pb = p.astype(BF16)
    kw = s.shape[1] // len(vlist)
    pv = _dot(pb[:, :kw], vlist[0])
    for r in range(1, len(vlist)):
        pv = pv + _dot(pb[:, r * kw:(r + 1) * kw], vlist[r])
    a_ref[g] = alpha * a_ref[g] + pv
    m_ref[g] = m_new


def _stack_heads(q, g, per, dh):
    return jnp.concatenate([q[:, (g * per + hl) * dh:(g * per + hl + 1) * dh] for hl in range(per)], axis=0)


def _unstack_heads(a_ref, l_ref, nkv, per, rows):
    outs = []
    for g in range(nkv):
        o = a_ref[g] / l_ref[g][:, :1]
        outs += [o[hl * rows:(hl + 1) * rows] for hl in range(per)]
    return jnp.concatenate(outs, axis=1)


def _dsa_attn_kernel(q_ref, k_ref, v_ref, bias_ref, o_ref, m_ref, l_ref, a_ref, *, tq, tw, nkv, per, dh):
    i = pl.program_id(1)
    n_tiles = (i * tq + tq + tw - 1) // tw
    m_ref[...] = jnp.full(m_ref.shape, NEG, F32)
    l_ref[...] = jnp.zeros(l_ref.shape, F32)
    a_ref[...] = jnp.zeros(a_ref.shape, F32)
    q = q_ref[...]
    qg = [_stack_heads(q, g, per, dh) for g in range(nkv)]

    def body(j, c):
        r0 = pl.multiple_of(j * tw, tw)
        bias = bias_ref[0, j].astype(F32)
        bias = jnp.concatenate([bias] * per, axis=0)
        for g in range(nkv):
            kb = k_ref[pl.ds(r0, tw), g * dh:(g + 1) * dh]
            vb = v_ref[pl.ds(r0, tw), g * dh:(g + 1) * dh]
            _gqa_update(g, _dot_nt(qg[g], kb) + bias, [vb], m_ref, l_ref, a_ref)
        return c

    lax.fori_loop(0, n_tiles, body, 0)
    o_ref[...] = _unstack_heads(a_ref, l_ref, nkv, per, tq).astype(o_ref.dtype)


def _dsa_attn_prompt(q, kb, vb, bias, b, t, nkv, tq, tw):
    m, w = q.shape
    dh = kb.shape[1] // nkv
    per = w // dh // nkv
    nq = t // tq
    nt = t // tw
    return pl.pallas_call(
        functools.partial(_dsa_attn_kernel, tq=tq, tw=tw, nkv=nkv, per=per, dh=dh),
        grid=(b, nq),
        in_specs=[pl.BlockSpec((tq, w), lambda bb, i: (bb * nq + i, 0)),
                  pl.BlockSpec((t, nkv * dh), lambda bb, i: (bb, 0)),
                  pl.BlockSpec((t, nkv * dh), lambda bb, i: (bb, 0)),
                  pl.BlockSpec((1, nt, tq, tw), lambda bb, i: (bb * nq + i, 0, 0, 0))],
        out_specs=pl.BlockSpec((tq, w), lambda bb, i: (bb * nq + i, 0)),
        out_shape=jax.ShapeDtypeStruct((m, w), BF16),
        scratch_shapes=[pltpu.VMEM((nkv, per * tq, LANES), F32), pltpu.VMEM((nkv, per * tq, LANES), F32),
                        pltpu.VMEM((nkv, per * tq, dh), F32)],
        compiler_params=_cp(("arbitrary", "arbitrary")),
        name="dsa_attn",
    )(q, kb, vb, bias)


def _dsa_dec_scores_kernel(pt_ref, iq_ref, iw_ref, ikn_ref, *rest, G, td, nhi, page, wnew):
    ikp = rest[:G]
    past_ref, new_ref = rest[G:]
    iq = iq_ref[0].astype(BF16)
    iw = iw_ref[0]

    def scores(ik):
        s = jnp.maximum(_dot_nt(iq, ik), 0.0) * iw
        out = s[0:td]
        for h in range(1, nhi):
            out = out + s[h * td:(h + 1) * td]
        return out

    past_ref[0] = jnp.concatenate([scores(ikp[r][0].astype(BF16)) for r in range(G)], axis=1)

    @pl.when(pl.program_id(1) == pl.num_programs(1) - 1)
    def _():
        sn = scores(_pad_rows(ikn_ref[...], wnew).astype(BF16))
        row = lax.broadcasted_iota(I32, sn.shape, 0)
        col = lax.broadcasted_iota(I32, sn.shape, 1)
        new_ref[0] = jnp.where(col <= row, sn, -jnp.inf)


def _dsa_dec_scores(pt, iq_hm, iw_hm, ik_new, pool_ik, base, bd, td, nhi, G, wnew):
    di = pool_ik.shape[2]
    page = pool_ik.shape[1]
    npg = pt.shape[1] // G
    seq3 = lambda shape: pl.BlockSpec((1,) + shape, lambda s, p, pt_: (s, 0, 0))
    grid_spec = pltpu.PrefetchScalarGridSpec(
        num_scalar_prefetch=1,
        grid=(bd, npg),
        in_specs=[seq3((nhi * td, di)), seq3((nhi * td, 1)), pl.BlockSpec((td, di), lambda s, p, pt_: (s, 0))]
                 + [pl.BlockSpec((1, page, di), functools.partial(lambda s, p, pt_, r: (base + pt_[s, p * G + r], 0, 0), r=r))
                    for r in range(G)],
        out_specs=[pl.BlockSpec((1, td, G * page), lambda s, p, pt_: (s, 0, p)), seq3((td, wnew))],
    )
    return pl.pallas_call(
        functools.partial(_dsa_dec_scores_kernel, G=G, td=td, nhi=nhi, page=page, wnew=wnew),
        grid_spec=grid_spec,
        out_shape=[jax.ShapeDtypeStruct((bd, td, pt.shape[1] * page), F32), jax.ShapeDtypeStruct((bd, td, wnew), F32)],
        compiler_params=_cp(("arbitrary", "arbitrary")),
        name="dsa_dec_scores",
    )(pt, iq_hm, iw_hm, ik_new, *([pool_ik] * G))


def _dsa_dec_select_kernel(sc_ref, bias_ref, keys_ref, *, topk, idx_bits):
    nt, rows, tw = keys_ref.shape
    for j in range(nt):
        sc = sc_ref[j]
        keys_ref[j] = jnp.where(sc == -jnp.inf, INT_MIN, _sort_key(sc))
    thr, cut = _select_topk(keys_ref, nt, topk, idx_bits)
    for j in range(nt):
        bias_ref[j] = _selection_bias(keys_ref[j], j, tw, thr, cut)


def _dsa_dec_select(sc_tiles, topk, rows):
    nt, m, tw = sc_tiles.shape
    spec = pl.BlockSpec((nt, rows, tw), lambda i: (0, i, 0))
    return pl.pallas_call(
        functools.partial(_dsa_dec_select_kernel, topk=topk, idx_bits=max(1, (nt * tw).bit_length())),
        grid=(m // rows,),
        in_specs=[spec],
        out_specs=spec,
        out_shape=jax.ShapeDtypeStruct((nt, m, tw), F32),
        scratch_shapes=[pltpu.VMEM((nt, rows, tw), I32)],
        compiler_params=_cp(("arbitrary",)),
        name="dsa_dec_select",
    )(sc_tiles)


def _dsa_dec_attn_kernel(pt_ref, q_ref, kn_ref, vn_ref, bias_ref, bnew_ref, *rest, G, td, nkv, per, dh, page):
    kp, vp = rest[:G], rest[G:2 * G]
    o_ref, m_ref, l_ref, a_ref = rest[2 * G:]
    p = pl.program_id(1)
    tw = bias_ref.shape[2]

    @pl.when(p == 0)
    def _():
        m_ref[...] = jnp.full(m_ref.shape, NEG, F32)
        l_ref[...] = jnp.zeros(l_ref.shape, F32)
        a_ref[...] = jnp.zeros(a_ref.shape, F32)

    q = q_ref[...].astype(BF16)
    qg = [_stack_heads(q, g, per, dh) for g in range(nkv)]
    bias = jnp.concatenate([bias_ref[j] for j in range(bias_ref.shape[0])], axis=1)
    bias = jnp.concatenate([bias] * per, axis=0)
    for g in range(nkv):
        sl = slice(g * dh, (g + 1) * dh)
        s = jnp.concatenate([_dot_nt(qg[g], kp[r][0, :, sl].astype(BF16)) for r in range(G)], axis=1)
        _gqa_update(g, s + bias, [vp[r][0, :, sl].astype(BF16) for r in range(G)], m_ref, l_ref, a_ref)

    @pl.when(p == pl.num_programs(1) - 1)
    def _():
        kn = _pad_rows(kn_ref[...], page).astype(BF16)
        vn = _pad_rows(vn_ref[...], page).astype(BF16)
        bias_n = jnp.concatenate([bnew_ref[0, :, 0:page]] * per, axis=0)
        for g in range(nkv):
            sl = slice(g * dh, (g + 1) * dh)
            _gqa_update(g, _dot_nt(qg[g], kn[:, sl]) + bias_n, [vn[:, sl]], m_ref, l_ref, a_ref)
        o_ref[...] = _unstack_heads(a_ref, l_ref, nkv, per, td)


def _dsa_dec_attn(pt, q, kn, vn, bias_tiles, pool_k, pool_v, base, bd, td, nkv, G):
    m, w = q.shape
    page = pool_k.shape[1]
    kvw = pool_k.shape[2]
    dh = kvw // nkv
    per = w // dh // nkv
    npg = pt.shape[1] // G
    nt, _, tw = bias_tiles.shape
    tiles_per_step = G * page // tw
    seq = lambda width: pl.BlockSpec((td, width), lambda s, p, pt_: (s, 0))

    def page_spec(r):
        return pl.BlockSpec((1, page, kvw), lambda s, p, pt_: (base + pt_[s, p * G + r], 0, 0))

    grid_spec = pltpu.PrefetchScalarGridSpec(
        num_scalar_prefetch=1,
        grid=(bd, npg),
        in_specs=[seq(w), seq(kvw), seq(kvw),
                  pl.BlockSpec((tiles_per_step, td, tw), lambda s, p, pt_: (p, s, 0)),
                  pl.BlockSpec((1, td, tw), lambda s, p, pt_: (nt - 1, s, 0))]
                 + [page_spec(r) for r in range(G)] * 2,
        out_specs=seq(w),
        scratch_shapes=[pltpu.VMEM((nkv, per * td, LANES), F32), pltpu.VMEM((nkv, per * td, LANES), F32),
                        pltpu.VMEM((nkv, per * td, dh), F32)],
    )
    return pl.pallas_call(
        functools.partial(_dsa_dec_attn_kernel, G=G, td=td, nkv=nkv, per=per, dh=dh, page=page),
        grid_spec=grid_spec,
        out_shape=jax.ShapeDtypeStruct((m, w), F32),
        compiler_params=_cp(("arbitrary", "arbitrary")),
        name="dsa_dec_attn",
    )(pt, q, kn, vn, bias_tiles, bias_tiles, *([pool_k] * G), *([pool_v] * G))


def kernel(x_prompt, x_sample, cache_diff_k, cache_diff_v, state_hgrn, cache_dsa_k, cache_dsa_v, cache_dsa_ik,
           cache_mem_k, cache_mem_v, state_ffn_conv, page_table, mem_prompt, norm_mix, w_in_ab, diff_lq1, diff_lk1,
           diff_lq2, diff_lk2, diff_subln, hgrn_lb_logits, hgrn_onorm, w_out_ab, w_in_c, idx_k_norm, w_out_c,
           norm_x, norm_mem, w_xq, w_xk, w_xv, w_xo, norm_ffn, w_ffn_in, conv_ffn_w, conv_ffn_b, w_ffn_out,
           norm_final):
    b, t, d = x_prompt.shape
    bd, td, _ = x_sample.shape
    depth = norm_mix.shape[0]
    n_ab = w_in_ab.shape[0]
    ha, dva = cache_diff_v.shape[3], cache_diff_v.shape[4]
    dqa = cache_diff_k.shape[4] // 2
    hb, dkb = state_hgrn.shape[2], state_hgrn.shape[3]
    kvc, dhc = cache_dsa_k.shape[3], cache_dsa_k.shape[4]
    hc = d // dhc
    di = cache_dsa_ik.shape[3]
    hi = w_in_c.shape[2] - (hc + 2 * kvc) * dhc - di
    hi = hi // (di + 1)
    hx = cache_mem_k.shape[3]
    n_mem = mem_prompt.shape[1]
    dff = w_ffn_out.shape[1]
    n_pool, page = cache_diff_k.shape[1], cache_diff_k.shape[2]
    past = page_table.shape[1] * page
    mp, ms = b * t, bd * td
    G = 8

    pos_p = jnp.tile(jnp.arange(t), b)
    pos_s = past + jnp.tile(jnp.arange(td), bd)
    tab64_p, half64 = _rope_tables(pos_p, dqa)
    tab64_s, _ = _rope_tables(pos_s, dqa)
    tab128_p, half128 = _rope_tables(pos_p, dhc)
    tab128_s, _ = _rope_tables(pos_s, dhc)

    hgrn_lb = jnp.cumsum(jax.nn.softmax(hgrn_lb_logits.astype(F32), axis=0), axis=0)[:n_ab]
    xp = x_prompt.reshape(mp, d)
    xs = x_sample.reshape(ms, d)
    gfin = norm_final

    mkf, mvf, mkb, mvb = _memproj(mem_prompt.reshape(b * n_mem, d), norm_mem, w_xk.astype(BF16), w_xv.astype(BF16), 256)

    new_p = {k: [] for k in ('diff_k', 'diff_v', 'hgrn', 'dsa_k', 'dsa_v', 'dsa_ik', 'conv')}
    new_s = {k: [] for k in new_p}

    for l in range(depth):
        j = l // 2
        wq_b, wo_b = w_xq[l].astype(BF16), w_xo[l].astype(BF16)
        if l % 2 == 0:
            w_in = w_in_ab[j].astype(BF16)
            w_out = w_out_ab[j].astype(BF16)
            wa, wb = w_out[:ha * dva], w_out[ha * dva:]
            lam_init = 0.8 - 0.6 * math.exp(-0.3 * l)
            lam = (jnp.exp(jnp.sum(diff_lq1[j].astype(F32) * diff_lk1[j].astype(F32)))
                   - jnp.exp(jnp.sum(diff_lq2[j].astype(F32) * diff_lk2[j].astype(F32))) + lam_init)
            q, kf, vf, kb, vb, qh, kk, lf, ib, gb = _inproj_ab(
                xp, norm_mix[l], w_in, hgrn_lb[j], tab64_p, half64, dqa, dkb, BF16, 256)
            o_a = _diffattn_prompt(lam, q, kb, vb, diff_subln[j], b, t, ha, dqa, 1.0 - lam_init, 256)
            o_b, s_new = _hgrn(qh, kk, ib, lf, gb, hgrn_onorm[j], None, 0, b, t, hb, 512, 128, 16, BF16)
            new_p['diff_k'].append(kf.reshape(b, t, ha, 2 * dqa))
            new_p['diff_v'].append(vf.reshape(b, t, ha, dva))
            new_p['hgrn'].append(s_new)
            parts_p = [(o_a, wa), (o_b, wb)]
            q, kf, vf, kb, vb, qh, kk, lf, ib, gb = _inproj_ab(
                xs, norm_mix[l], w_in, hgrn_lb[j], tab64_s, half64, dqa, dkb, F32, ms)
            pool_k = cache_diff_k.reshape(-1, page, ha * 2 * dqa)
            pool_v = cache_diff_v.reshape(-1, page, ha * dva)
            o_a = _diff_decode(page_table, lam, q, kf, vf, diff_subln[j], pool_k, pool_v, j * n_pool, bd, td, ha,
                               dqa, 1.0 - lam_init, G)
            c_s = math.gcd(td, 64)
            o_b, s_new = _hgrn(qh, kk, ib, lf, gb, hgrn_onorm[j], state_hgrn.reshape(-1, hb, dkb, dkb), j * bd,
                               bd, td, hb, td, c_s, min(16, c_s), F32)
            new_s['diff_k'].append(kf.reshape(bd, td, ha, 2 * dqa))
            new_s['diff_v'].append(vf.reshape(bd, td, ha, dva))
            new_s['hgrn'].append(s_new)
            parts_s = [(o_a, wa), (o_b, wb)]
        else:
            w_in = w_in_c[j].astype(BF16)
            cw = w_in.shape[1]
            main = (hc + 2 * kvc) * dhc + hi * di
            w_pad = jnp.concatenate([w_in, jnp.zeros((d, main + LANES - cw), BF16)], axis=1)
            ikg_pad = jnp.concatenate([idx_k_norm[j].astype(F32), jnp.zeros((LANES - di,), F32)]).reshape(1, LANES)
            dims = (hc * dhc, kvc * dhc, hi * di, di, hi, dhc)
            w_out = w_out_c[j].astype(BF16)
            q, kf, vf, kb, vb, iq, tail, ik2 = _inproj_c(
                xp, norm_mix[l], w_pad, ikg_pad, tab128_p, tab64_p, half128, half64, dims, BF16, 256)
            topk = min(DSA_TOPK_MAX, t // 4)
            bias = _dsa_index_prompt(iq, tail, ik2, b, t, topk, hi, di, 128, 512)
            o_c = _dsa_attn_prompt(q, kb, vb, bias, b, t, kvc, 128, 512)
            new_p['dsa_k'].append(kf.reshape(b, t, kvc, dhc))
            new_p['dsa_v'].append(vf.reshape(b, t, kvc, dhc))
            new_p['dsa_ik'].append(tail[:, :di].reshape(b, t, di))
            parts_p = [(o_c, w_out)]
            q, kf, vf, kb, vb, iq, tail, ik2 = _inproj_c(
                xs, norm_mix[l], w_pad, ikg_pad, tab128_s, tab64_s, half128, half64, dims, F32, ms)
            tw = 512
            iq_hm = iq.reshape(bd, td, hi, di).transpose(0, 2, 1, 3).reshape(bd, hi * td, di)
            iw_hm = tail[:, di:di + hi].reshape(bd, td, hi).transpose(0, 2, 1).reshape(bd, hi * td, 1)
            ik_new = tail[:, :di]
            sc_past, sc_new = _dsa_dec_scores(page_table, iq_hm, iw_hm, ik_new, cache_dsa_ik.reshape(-1, page, di),
                                              j * n_pool, bd, td, hi, G, tw)
            sc = jnp.concatenate([sc_past, sc_new], axis=2).reshape(ms, -1)
            nt = sc.shape[1] // tw
            sc_tiles = sc.reshape(ms, nt, tw).transpose(1, 0, 2)
            topk = min(DSA_TOPK_MAX, (past + td) // 4)
            bias_tiles = _dsa_dec_select(sc_tiles, topk, 128 if ms % 128 == 0 else ms)
            pool_k = cache_dsa_k.reshape(-1, page, kvc * dhc)
            pool_v = cache_dsa_v.reshape(-1, page, kvc * dhc)
            o_c = _dsa_dec_attn(page_table, q, kf, vf, bias_tiles, pool_k, pool_v, j * n_pool, bd, td, kvc, G)
            new_s['dsa_k'].append(kf.reshape(bd, td, kvc, dhc))
            new_s['dsa_v'].append(vf.reshape(bd, td, kvc, dhc))
            new_s['dsa_ik'].append(ik_new.reshape(bd, td, di))
            parts_s = [(o_c, w_out)]

        last = l == depth - 1
        wi_b, wo2_b = w_ffn_in[l].astype(BF16), w_ffn_out[l].astype(BF16)
        xp = _cross_prompt(xp, parts_p, norm_x[l], wq_b, wo_b, mkb[l].reshape(b, n_mem, d), mvb[l].reshape(b, n_mem, d),
                           hx, t, 256)
        xp, conv_p = _ffn(xp, norm_ffn[l], wi_b, conv_ffn_w[l], conv_ffn_b[l], wo2_b, gfin, last, t, 256)
        new_p['conv'].append(conv_p)
        xs = _cross_sample(xs, parts_s, norm_x[l], wq_b, wo_b, cache_mem_k.reshape(depth * bd, n_mem, d),
                           cache_mem_v.reshape(depth * bd, n_mem, d), l * bd, hx, bd, td)
        st = state_ffn_conv[l]
        zeros = jnp.zeros((bd, td - 1, dff), F32)
        fill1 = jnp.concatenate([st[:, 1:2], zeros], axis=1).reshape(ms, dff)
        fill2 = jnp.concatenate([st, zeros[:, 1:]], axis=1).reshape(ms, dff)
        xs, gate_s = _ffn(xs, norm_ffn[l], wi_b, conv_ffn_w[l], conv_ffn_b[l], wo2_b, gfin, last, td, ms,
                          fills=(fill1, fill2))
        new_s['conv'].append(gate_s.reshape(bd, td, dff)[:, td - 2:])

    sp = {k: jnp.stack(v, axis=0) for k, v in new_p.items()}
    ss = {k: jnp.stack(v, axis=0) for k, v in new_s.items()}
    mk_p = mkf.reshape(depth, b, n_mem, hx, d // hx)
    mv_p = mvf.reshape(depth, b, n_mem, hx, d // hx)
    return (xp.reshape(b, t, d), xs.reshape(bd, td, d),
            sp['diff_k'], sp['diff_v'], sp['hgrn'], sp['dsa_k'], sp['dsa_v'], sp['dsa_ik'],
            mk_p, mv_p, sp['conv'],
            ss['diff_k'], ss['diff_v'], ss['hgrn'], ss['dsa_k'], ss['dsa_v'], ss['dsa_ik'], ss['conv'])
```

```python
import functools
import math

import jax
import jax.numpy as jnp
import numpy as np
from jax import lax
from jax.experimental import pallas as pl
from jax.experimental.pallas import tpu as pltpu

F32 = jnp.float32
BF16 = jnp.bfloat16
I32 = jnp.int32

EPS = 1e-6
ROPE_THETA = 500000.0
ROT_DIV = 4
DSA_TOPK_MAX = 256
NEG = -1e30
INT_MIN = -2147483648
LANES = 128
VMEM_LIMIT = 56 * 1024 * 1024

NT = (((1,), (1,)), ((), ()))


def _cp(sem):
    return pltpu.CompilerParams(dimension_semantics=sem, vmem_limit_bytes=VMEM_LIMIT)


def _dot(a, b):
    return jnp.dot(a, b, preferred_element_type=F32)


def _dot_nt(a, b):
    return lax.dot_general(a, b, NT, preferred_element_type=F32)


def _rms(x, g):
    ms = jnp.mean(x * x, axis=-1, keepdims=True)
    return x * lax.rsqrt(ms + EPS) * g


def _silu(x):
    return x * jax.nn.sigmoid(x)


def _tile_lanes(t, n):
    return t if n == 1 else jnp.concatenate([t] * n, axis=1)


def _rope(z, c, s1, s2, half):
    w = z.shape[1]
    n = w // LANES
    return (z * _tile_lanes(c, n) + pltpu.roll(z, half, 1) * _tile_lanes(s1, n)
            + pltpu.roll(z, w - half, 1) * _tile_lanes(s2, n))


def _rope_tables(pos, head_dim):
    half = head_dim // ROT_DIV // 2
    inv = jnp.exp(jnp.arange(half, dtype=F32) * (-math.log(ROPE_THETA) / half))
    ang = pos.astype(F32)[:, None] * inv[None, :]
    cos, sin = jnp.cos(ang), jnp.sin(ang)
    m = pos.shape[0]
    rest = head_dim - 2 * half
    c = jnp.concatenate([cos, cos, jnp.ones((m, rest), F32)], axis=1)
    s1 = jnp.concatenate([jnp.zeros((m, half), F32), sin, jnp.zeros((m, rest), F32)], axis=1)
    s2 = jnp.concatenate([-sin, jnp.zeros((m, half + rest), F32)], axis=1)
    rep = LANES // head_dim
    return tuple(jnp.tile(t, (1, rep)) for t in (c, s1, s2)), half


def _store_heads(ref, x):
    dim = ref.shape[2]
    for h in range(ref.shape[1]):
        ref[:, h, :] = x[:, h * dim:(h + 1) * dim]


def _const_spec(shape):
    nd = len(shape)
    return pl.BlockSpec(shape, lambda *_: (0,) * nd)


def _inproj_ab_kernel(x_ref, g_ref, w_ref, lb_ref, c_ref, s1_ref, s2_ref,
                      q_ref, kf_ref, vf_ref, kb_ref, vb_ref, qh_ref, kk_ref, lf_ref, ib_ref, gb_ref,
                      *, half, qscale, hscale):
    h = _rms(x_ref[...], g_ref[...]).astype(BF16)
    sw = q_ref.shape[1]

    def seg(i):
        return _dot(h, w_ref[:, i * sw:(i + 1) * sw])

    c, s1, s2 = c_ref[...], s1_ref[...], s2_ref[...]
    q_ref[...] = (_rope(seg(0), c, s1, s2, half) * qscale).astype(q_ref.dtype)
    ka = _rope(seg(1), c, s1, s2, half)
    _store_heads(kf_ref, ka)
    kb_ref[...] = ka.astype(BF16)
    va = seg(2)
    _store_heads(vf_ref, va)
    vb_ref[...] = va.astype(BF16)
    qh_ref[...] = _silu(seg(3)) * hscale
    lb = lb_ref[...]
    fg = lb + (1.0 - lb) * jax.nn.sigmoid(seg(4))
    kk_ref[...] = 1.0 - fg
    lf_ref[...] = jnp.log(fg)
    ib_ref[...] = seg(5)
    gb_ref[...] = _silu(seg(6))


def _inproj_ab(x, g, w, lb, tabs, half, dqa, dkb, qdtype, tm):
    m, d = x.shape
    sw = w.shape[1] // 7
    nh = sw // (2 * dqa)
    row = lambda width: pl.BlockSpec((tm, width), lambda i: (i, 0))
    heads = pl.BlockSpec((tm, nh, 2 * dqa), lambda i: (i, 0, 0))
    outs = [(sw, qdtype), (sw, F32), (sw, F32), (sw, BF16), (sw, BF16)] + [(sw, F32)] * 5
    out_specs = [row(wd) for wd, _ in outs]
    out_shape = [jax.ShapeDtypeStruct((m, wd), dt) for wd, dt in outs]
    for i in (1, 2):
        out_specs[i] = heads
        out_shape[i] = jax.ShapeDtypeStruct((m, nh, 2 * dqa), F32)
    return pl.pallas_call(
        functools.partial(_inproj_ab_kernel, half=half, qscale=dqa ** -0.5, hscale=dkb ** -0.5),
        grid=(m // tm,),
        in_specs=[row(d), _const_spec((1, d)), _const_spec(w.shape), _const_spec((1, sw)),
                  row(LANES), row(LANES), row(LANES)],
        out_specs=out_specs,
        out_shape=out_shape,
        compiler_params=_cp(("arbitrary",)),
        name="inproj_ab",
    )(x, g.reshape(1, d), w, lb.reshape(1, sw), *tabs)


def _inproj_c_kernel(x_ref, g_ref, w_ref, ikg_ref, c128_ref, s1128_ref, s2128_ref, c64_ref, s164_ref, s264_ref,
                     q_ref, kf_ref, vf_ref, kb_ref, vb_ref, iq_ref, tail_ref, ik2_ref,
                     *, nq, nkv, niq, di, half128, half64, qscale, iwscale):
    h = _rms(x_ref[...], g_ref[...]).astype(BF16)
    c1, a1, b1 = c128_ref[...], s1128_ref[...], s2128_ref[...]
    c6, a6, b6 = c64_ref[...], s164_ref[...], s264_ref[...]
    o = 0
    q = _dot(h, w_ref[:, o:o + nq]); o += nq
    q_ref[...] = (_rope(q, c1, a1, b1, half128) * qscale).astype(q_ref.dtype)
    k = _rope(_dot(h, w_ref[:, o:o + nkv]), c1, a1, b1, half128); o += nkv
    _store_heads(kf_ref, k)
    kb_ref[...] = k.astype(BF16)
    v = _dot(h, w_ref[:, o:o + nkv]); o += nkv
    _store_heads(vf_ref, v)
    vb_ref[...] = v.astype(BF16)
    iq = _rope(_dot(h, w_ref[:, o:o + niq]), c6, a6, b6, half64); o += niq
    iq_ref[...] = iq.astype(iq_ref.dtype)
    t = _dot(h, w_ref[:, o:o + LANES])
    lane = lax.broadcasted_iota(I32, t.shape, 1)
    is_k = lane < di
    tk = jnp.where(is_k, t, 0.0)
    ms = jnp.sum(tk * tk, axis=-1, keepdims=True) * (1.0 / di)
    ikn = tk * lax.rsqrt(ms + EPS) * ikg_ref[...]
    ikr = _rope(ikn, c6, a6, b6, half64)
    ikr = jnp.where(is_k, ikr, 0.0)
    tail_ref[...] = jnp.where(is_k, ikr, t * iwscale)
    ik2_ref[...] = (ikr + pltpu.roll(ikr, di, 1)).astype(BF16)


def _inproj_c(x, g, w_pad, ikg_pad, tabs128, tabs64, half128, half64, dims, qdtype, tm):
    m, d = x.shape
    nq, nkv, niq, di, hi, dhc = dims
    row = lambda width: pl.BlockSpec((tm, width), lambda i: (i, 0))
    outs = [(nq, qdtype), (nkv, F32), (nkv, F32), (nkv, BF16), (nkv, BF16), (niq, qdtype), (LANES, F32), (LANES, BF16)]
    out_specs = [row(wd) for wd, _ in outs]
    out_shape = [jax.ShapeDtypeStruct((m, wd), dt) for wd, dt in outs]
    for i in (1, 2):
        out_specs[i] = pl.BlockSpec((tm, nkv // dhc, dhc), lambda i_: (i_, 0, 0))
        out_shape[i] = jax.ShapeDtypeStruct((m, nkv // dhc, dhc), F32)
    return pl.pallas_call(
        functools.partial(_inproj_c_kernel, nq=nq, nkv=nkv, niq=niq, di=di, half128=half128, half64=half64,
                          qscale=dhc ** -0.5, iwscale=(hi * di) ** -0.5),
        grid=(m // tm,),
        in_specs=[row(d), _const_spec((1, d)), _const_spec(w_pad.shape), _const_spec((1, LANES))]
                 + [row(LANES)] * 6,
        out_specs=out_specs,
        out_shape=out_shape,
        compiler_params=_cp(("arbitrary",)),
        name="inproj_c",
    )(x, g.reshape(1, d), w_pad, ikg_pad, *tabs128, *tabs64)


def _memproj_kernel(x_ref, g_ref, wk_ref, wv_ref, kf_ref, vf_ref, kb_ref, vb_ref):
    h = _rms(x_ref[...], g_ref[0]).astype(BF16)
    k = _dot(h, wk_ref[0])
    v = _dot(h, wv_ref[0])
    kf_ref[0] = k
    vf_ref[0] = v
    kb_ref[0] = k.astype(BF16)
    vb_ref[0] = v.astype(BF16)


def _memproj(x, g, wk, wv, tm):
    m, d = x.shape
    depth, _, n = wk.shape
    wspec = pl.BlockSpec((1, d, n), lambda l, i: (l, 0, 0))
    ospec = pl.BlockSpec((1, tm, n), lambda l, i: (l, i, 0))
    return pl.pallas_call(
        _memproj_kernel,
        grid=(depth, m // tm),
        in_specs=[pl.BlockSpec((tm, d), lambda l, i: (i, 0)), pl.BlockSpec((1, 1, d), lambda l, i: (l, 0, 0)),
                  wspec, wspec],
        out_specs=[ospec] * 4,
        out_shape=[jax.ShapeDtypeStruct((depth, m, n), dt) for dt in (F32, F32, BF16, BF16)],
        compiler_params=_cp(("arbitrary", "arbitrary")),
        name="memproj",
    )(x, g.reshape(depth, 1, d), wk, wv)


def _softmax_step(s, vb, m, l, a):
    m_new = jnp.maximum(m, jnp.max(s, axis=-1, keepdims=True))
    alpha = jnp.exp(m - m_new)
    p = jnp.exp(s - m_new)
    l = alpha * l + jnp.sum(p, axis=-1, keepdims=True)
    a = alpha * a + _dot(p.astype(BF16), vb)
    return m_new, l, a


def _diff_finish(a1, l1, a2, l2, lam, g, out_scale):
    o = a1 / l1 - lam * (a2 / l2)
    return _rms(o, g) * out_scale


def _diffattn_kernel(lam_ref, q_ref, k_ref, v_ref, g_ref, o_ref, *, tq, dq, out_scale):
    i = pl.program_id(2)
    q = q_ref[...]
    lane = lax.broadcasted_iota(I32, q.shape, 1)
    zero = jnp.zeros_like(q)
    q1 = jnp.where(lane < dq, q, zero)
    q2 = jnp.where(lane >= dq, q, zero)
    dv = v_ref.shape[1]
    causal = (lax.broadcasted_iota(I32, (tq, tq), 0) >= lax.broadcasted_iota(I32, (tq, tq), 1))
    causal_bias = jnp.where(causal, 0.0, NEG)

    def scores(j):
        kb = k_ref[pl.ds(pl.multiple_of(j * tq, tq), tq), :]
        bias = causal_bias * (j == i).astype(F32)
        return _dot_nt(q1, kb) + bias, _dot_nt(q2, kb) + bias

    def attend(j, s1, s2, state):
        vb = v_ref[pl.ds(pl.multiple_of(j * tq, tq), tq), :]
        m1, l1, a1, m2, l2, a2 = state
        return _softmax_step(s1, vb, m1, l1, a1) + _softmax_step(s2, vb, m2, l2, a2)

    def init():
        return (jnp.full((tq, 1), NEG, F32), jnp.zeros((tq, 1), F32), jnp.zeros((tq, dv), F32))

    def body(j, carry):
        s1, s2, state = carry
        n1, n2 = scores(j + 1)
        return n1, n2, attend(j, s1, s2, state)

    s1, s2, state = lax.fori_loop(0, i, body, scores(0) + (init() + init(),))
    m1, l1, a1, m2, l2, a2 = attend(i, s1, s2, state)
    o_ref[...] = _diff_finish(a1, l1, a2, l2, lam_ref[0], g_ref[...], out_scale).astype(o_ref.dtype)


def _diffattn_prompt(lam, q, kb, vb, g, b, t, nh, dq, out_scale, tq):
    m, w = q.shape
    dv = w // nh
    nq = t // tq
    smem = pl.BlockSpec(memory_space=pltpu.SMEM)
    return pl.pallas_call(
        functools.partial(_diffattn_kernel, tq=tq, dq=dq, out_scale=out_scale),
        grid=(b, nh, nq),
        in_specs=[smem,
                  pl.BlockSpec((tq, dv), lambda bb, h, i: (bb * nq + i, h)),
                  pl.BlockSpec((t, dv), lambda bb, h, i: (bb, h)),
                  pl.BlockSpec((t, dv), lambda bb, h, i: (bb, h)),
                  _const_spec((1, dv))],
        out_specs=pl.BlockSpec((tq, dv), lambda bb, h, i: (bb * nq + i, h)),
        out_shape=jax.ShapeDtypeStruct((m, w), BF16),
        compiler_params=_cp(("arbitrary",) * 3),
        name="diffattn_prompt",
    )(lam.reshape(1), q, kb, vb, g.reshape(1, dv))


def _pad_rows(x, rows):
    return jnp.concatenate([x, jnp.zeros((rows - x.shape[0], x.shape[1]), x.dtype)], axis=0)


def _diff_decode_kernel(pt_ref, lam_ref, q_ref, kn_ref, vn_ref, g_ref, *rest, G, nh, dq, out_scale, page):
    kp, vp = rest[:G], rest[G:2 * G]
    o_ref, m_ref, l_ref, a_ref = rest[2 * G:]
    p = pl.program_id(1)
    td = q_ref.shape[0]
    dv = q_ref.shape[1] // nh

    @pl.when(p == 0)
    def _():
        m_ref[...] = jnp.full(m_ref.shape, NEG, F32)
        l_ref[...] = jnp.zeros(l_ref.shape, F32)
        a_ref[...] = jnp.zeros(a_ref.shape, F32)

    q = q_ref[...].astype(BF16)
    lane = lax.broadcasted_iota(I32, (td, dv), 1)

    def q_pair(h):
        qh = q[:, h * dv:(h + 1) * dv]
        zero = jnp.zeros_like(qh)
        return jnp.concatenate([jnp.where(lane < dq, qh, zero), jnp.where(lane >= dq, qh, zero)], axis=0)

    def update(h, s, vlist):
        m_old = m_ref[h]
        m_new = jnp.maximum(m_old, jnp.max(s, axis=-1, keepdims=True))
        alpha = jnp.exp(m_old - m_new)
        pm = jnp.exp(s - m_new[:, :1]).astype(BF16)
        l_ref[h] = alpha * l_ref[h] + jnp.sum(pm.astype(F32), axis=-1, keepdims=True)
        pv = jnp.zeros((2 * td, dv), F32)
        for r, vb in enumerate(vlist):
            pv = pv + _dot(pm[:, r * page:(r + 1) * page], vb)
        a_ref[h] = alpha * a_ref[h] + pv
        m_ref[h] = m_new

    for h in range(nh):
        q2 = q_pair(h)
        s = jnp.concatenate([_dot_nt(q2, kp[r][:, h, :].astype(BF16)) for r in range(G)], axis=1)
        update(h, s, [vp[r][:, h, :].astype(BF16) for r in range(G)])

    @pl.when(p == pl.num_programs(1) - 1)
    def _():
        row = lax.broadcasted_iota(I32, (2 * td, page), 0)
        col = lax.broadcasted_iota(I32, (2 * td, page), 1)
        tt = jnp.where(row >= td, row - td, row)
        valid = col <= tt
        outs = []
        for h in range(nh):
            kn = _pad_rows(kn_ref[:, h, :], page).astype(BF16)
            vn = _pad_rows(vn_ref[:, h, :], page).astype(BF16)
            s = jnp.where(valid, _dot_nt(q_pair(h), kn), NEG)
            update(h, s, [vn])
            a, l = a_ref[h], l_ref[h]
            outs.append(_diff_finish(a[:td], l[:td], a[td:], l[td:], lam_ref[0], g_ref[...], out_scale))
        o_ref[...] = jnp.concatenate(outs, axis=1)


def _diff_decode(pt, lam, q, kn, vn, g, pool_k, pool_v, base, page, bd, td, nh, dq, out_scale, G):
    m, w = q.shape
    dv = w // nh
    npg = pt.shape[1] // G
    seq = pl.BlockSpec((td, w), lambda s, p, pt_: (s, 0))
    new = pl.BlockSpec((td, nh, dv), lambda s, p, pt_: (s, 0, 0))

    def page_spec(r):
        return pl.BlockSpec((page, nh, dv), lambda s, p, pt_: (base + pt_[s, p * G + r], 0, 0))

    grid_spec = pltpu.PrefetchScalarGridSpec(
        num_scalar_prefetch=1,
        grid=(bd, npg),
        in_specs=[pl.BlockSpec(memory_space=pltpu.SMEM), seq, new, new,
                  pl.BlockSpec((1, dv), lambda s, p, pt_: (0, 0))]
                 + [page_spec(r) for r in range(G)] * 2,
        out_specs=seq,
        scratch_shapes=[pltpu.VMEM((nh, 2 * td, LANES), F32), pltpu.VMEM((nh, 2 * td, LANES), F32),
                        pltpu.VMEM((nh, 2 * td, dv), F32)],
    )
    return pl.pallas_call(
        functools.partial(_diff_decode_kernel, G=G, nh=nh, dq=dq, out_scale=out_scale, page=page),
        grid_spec=grid_spec,
        out_shape=jax.ShapeDtypeStruct((m, w), F32),
        compiler_params=_cp(("arbitrary", "arbitrary")),
        name="diffattn_decode",
    )(pt, lam.reshape(1), q, kn, vn, g.reshape(1, dv), *([pool_k] * G), *([pool_v] * G))


def _cumsum_rows(x):
    n = x.shape[0]
    row = lax.broadcasted_iota(I32, x.shape, 0)
    s = 1
    while s < n:
        x = x + jnp.where(row >= s, pltpu.roll(x, s, 0), 0.0)
        s *= 2
    return x


def _block_rows(b, blk, edge, keep):
    n = b.shape[0]
    parts = []
    for p in range(n // blk):
        lo = p * blk
        if keep(p):
            r = lo - 1 if edge == 'prev_end' else lo + blk - 1
            parts.append(jnp.broadcast_to(b[r:r + 1], (blk, b.shape[1])))
        else:
            parts.append(b[lo:lo + blk])
    return jnp.concatenate(parts, axis=0)


def _hgrn_chunk(q, k, v, lf, st, sub):
    c, dk = q.shape
    b = _cumsum_rows(lf)
    o = _dot_nt((q * jnp.exp(b)).astype(BF16), st.astype(BF16))
    row_s = lax.broadcasted_iota(I32, (sub, dk), 0)
    parts = []
    for i in range(c // sub):
        lo = i * sub
        qi, ki, vi, bi = (a[lo:lo + sub] for a in (q, k, v, b))
        acc = jnp.zeros((sub, v.shape[1]), F32)
        for s in range(sub):
            d = jnp.where(row_s >= s, bi - bi[s:s + 1], -jnp.inf)
            a = jnp.sum(jnp.exp(d) * qi * ki[s:s + 1], axis=-1, keepdims=True)
            acc = acc + a * vi[s:s + 1]
        parts.append(acc)
    o = o + (parts[0] if len(parts) == 1 else jnp.concatenate(parts, axis=0))
    if c > sub:
        row = lax.broadcasted_iota(I32, (c, dk), 0)
        tt = lax.broadcasted_iota(I32, (c, c), 0)
        ss = lax.broadcasted_iota(I32, (c, c), 1)
        att = jnp.zeros((c, c), F32)
        blk = sub
        while blk < c:
            odd = (row // blk) % 2 == 1
            rq = _block_rows(b, blk, 'prev_end', lambda p: p % 2 == 1)
            rk = _block_rows(b, blk, 'end', lambda p: p % 2 == 0)
            qs = (q * jnp.exp(jnp.where(odd, b - rq, -jnp.inf))).astype(BF16)
            ks = (k * jnp.exp(jnp.where(odd, -jnp.inf, rk - b))).astype(BF16)
            att = att + jnp.where(tt // (2 * blk) == ss // (2 * blk), _dot_nt(qs, ks), 0.0)
            blk *= 2
        o = o + _dot(att.astype(BF16), v.astype(BF16))
    bl = b[c - 1:c]
    kd = k * jnp.exp(bl - b)
    if c % LANES:
        pad = -c % LANES
        v, kd = _pad_rows(v, c + pad), _pad_rows(kd, c + pad)
    st = st * jnp.exp(bl) + _dot(v.T.astype(BF16), kd.astype(BF16))
    return o, st


def _hgrn_kernel(*refs, c, sub, has_s0):
    if has_s0:
        q_ref, k_ref, v_ref, lf_ref, gb_ref, on_ref, s0_ref, o_ref, so_ref, st_ref = refs
    else:
        q_ref, k_ref, v_ref, lf_ref, gb_ref, on_ref, o_ref, so_ref, st_ref = refs
    i = pl.program_id(2)

    @pl.when(i == 0)
    def _():
        st_ref[...] = s0_ref[0, 0].T if has_s0 else jnp.zeros(st_ref.shape, F32)

    def body(ci, carry):
        r0 = pl.multiple_of(ci * c, c)
        sl = pl.ds(r0, c)
        o, st = _hgrn_chunk(q_ref[sl, :], k_ref[sl, :], v_ref[sl, :], lf_ref[sl, :], st_ref[...], sub)
        st_ref[...] = st
        o_ref[sl, :] = (_rms(o, on_ref[...]) * gb_ref[sl, :]).astype(o_ref.dtype)
        return carry

    lax.fori_loop(0, q_ref.shape[0] // c, body, 0)

    @pl.when(i == pl.num_programs(2) - 1)
    def _():
        so_ref[0, 0] = st_ref[...].T


def _hgrn(qh, kk, ib, lf, gb, onorm, s0, s0_base, b, t, nh, tc, c, sub, odtype):
    m, w = qh.shape
    dk = w // nh
    nt = t // tc
    blk = pl.BlockSpec((tc, dk), lambda bb, h, i: (bb * nt + i, h))
    st_spec = pl.BlockSpec((1, 1, dk, dk), lambda bb, h, i: (bb, h, 0, 0))
    ins = [qh, kk, ib, lf, gb, onorm.reshape(1, dk)]
    in_specs = [blk] * 5 + [_const_spec((1, dk))]
    if s0 is not None:
        ins.append(s0)
        in_specs.append(pl.BlockSpec((1, 1, dk, dk), lambda bb, h, i: (s0_base + bb, h, 0, 0)))
    return pl.pallas_call(
        functools.partial(_hgrn_kernel, c=c, sub=sub, has_s0=s0 is not None),
        grid=(b, nh, nt),
        in_specs=in_specs,
        out_specs=[blk, st_spec],
        out_shape=[jax.ShapeDtypeStruct((m, w), odtype), jax.ShapeDtypeStruct((b, nh, dk, dk), F32)],
        scratch_shapes=[pltpu.VMEM((dk, dk), F32)],
        compiler_params=_cp(("arbitrary",) * 3),
        name="hgrn",
    )(*ins)


def _mixer_out(x, parts):
    for a_ref, w_ref in parts:
        x = x + _dot(a_ref[...].astype(BF16), w_ref[...])
    return x


def _mem_attend(q, mk_head, mv_head, nh):
    dh = q.shape[1] // nh
    outs = []
    for h in range(nh):
        s = _dot_nt(q[:, h * dh:(h + 1) * dh], mk_head(h))
        p = jnp.exp(s - jnp.max(s, axis=-1, keepdims=True))
        p = p / jnp.sum(p, axis=-1, keepdims=True)
        outs.append(_dot(p.astype(BF16), mv_head(h)))
    return jnp.concatenate(outs, axis=1).astype(BF16)


def _cross_prompt_kernel(*refs, n_parts, nh, qscale):
    x_ref = refs[0]
    parts = [(refs[1 + 2 * j], refs[2 + 2 * j]) for j in range(n_parts)]
    g_ref, wq_ref, wo_ref, mk_ref, mv_ref, o_ref = refs[1 + 2 * n_parts:]
    x1 = _mixer_out(x_ref[...], parts)
    h = _rms(x1, g_ref[...]).astype(BF16)
    q = (_dot(h, wq_ref[...]) * qscale).astype(BF16)
    dh = q.shape[1] // nh
    o = _mem_attend(q, lambda h: mk_ref[0, :, h * dh:(h + 1) * dh], lambda h: mv_ref[0, :, h * dh:(h + 1) * dh], nh)
    o_ref[...] = x1 + _dot(o, wo_ref[...])


def _cross_prompt(x, parts, g, wq, wo, mk, mv, nh, t, tm):
    m, d = x.shape
    n_mem = mk.shape[1]
    per = t // tm
    row = lambda width: pl.BlockSpec((tm, width), lambda i: (i, 0))
    mem = pl.BlockSpec((1, n_mem, d), lambda i: (i // per, 0, 0))
    ins, in_specs = [x], [row(d)]
    for a, w in parts:
        ins += [a, w]
        in_specs += [row(a.shape[1]), _const_spec(w.shape)]
    ins += [g.reshape(1, d), wq, wo, mk, mv]
    in_specs += [_const_spec((1, d)), _const_spec(wq.shape), _const_spec(wo.shape), mem, mem]
    return pl.pallas_call(
        functools.partial(_cross_prompt_kernel, n_parts=len(parts), nh=nh, qscale=(d // nh) ** -0.5),
        grid=(m // tm,),
        in_specs=in_specs,
        out_specs=row(d),
        out_shape=jax.ShapeDtypeStruct((m, d), F32),
        compiler_params=_cp(("arbitrary",)),
        name="cross_prompt",
    )(*ins)


def _cross_sample_kernel(*refs, n_parts, nh, qscale, td):
    x_ref = refs[0]
    parts = [(refs[1 + 2 * j], refs[2 + 2 * j]) for j in range(n_parts)]
    g_ref, wq_ref, wo_ref, mk_ref, mv_ref, o_ref, x1_ref, q_ref, a_ref = refs[1 + 2 * n_parts:]
    s = pl.program_id(0)

    @pl.when(s == 0)
    def _():
        x1 = _mixer_out(x_ref[...], parts)
        x1_ref[...] = x1
        q_ref[...] = _dot(_rms(x1, g_ref[...]).astype(BF16), wq_ref[...]) * qscale

    sl = pl.ds(pl.multiple_of(s * td, td), td)
    a_ref[sl, :] = _mem_attend(q_ref[sl, :].astype(BF16), lambda h: mk_ref[:, h, :].astype(BF16),
                               lambda h: mv_ref[:, h, :].astype(BF16), nh).astype(F32)

    @pl.when(s == pl.num_programs(0) - 1)
    def _():
        o_ref[...] = x1_ref[...] + _dot(a_ref[...].astype(BF16), wo_ref[...])


def _cross_sample(x, parts, g, wq, wo, mk, mv, base, n_mem, nh, bd, td):
    m, d = x.shape
    mem = pl.BlockSpec((n_mem, nh, d // nh), lambda s: (base + s, 0, 0))
    ins, in_specs = [x], [_const_spec((m, d))]
    for a, w in parts:
        ins += [a, w]
        in_specs += [_const_spec(a.shape), _const_spec(w.shape)]
    ins += [g.reshape(1, d), wq, wo, mk, mv]
    in_specs += [_const_spec((1, d)), _const_spec(wq.shape), _const_spec(wo.shape), mem, mem]
    return pl.pallas_call(
        functools.partial(_cross_sample_kernel, n_parts=len(parts), nh=nh, qscale=(d // nh) ** -0.5, td=td),
        grid=(bd,),
        in_specs=in_specs,
        out_specs=_const_spec((m, d)),
        out_shape=jax.ShapeDtypeStruct((m, d), F32),
        scratch_shapes=[pltpu.VMEM((m, d), F32)] * 3,
        compiler_params=_cp(("arbitrary",)),
        name="cross_sample",
    )(*ins)


def _ffn_kernel(*refs, carry_mode, seq_rows, blocks_per_seq, nchunk, cw, final_norm):
    if carry_mode:
        (x_ref, g_ref, wi_ref, cwt_ref, cb_ref, wo_ref, gf_ref, o_ref, st_ref, prev_ref) = refs
    else:
        (x_ref, g_ref, wi_ref, cwt_ref, cb_ref, wo_ref, gf_ref, f1_ref, f2_ref, o_ref, gate_ref) = refs
    x = x_ref[...]
    tm = x.shape[0]
    dff = cwt_ref.shape[1]
    h = _rms(x, g_ref[...]).astype(BF16)
    row = lax.broadcasted_iota(I32, (tm, cw), 0)
    if carry_mode:
        @pl.when(pl.program_id(0) % blocks_per_seq == 0)
        def _():
            prev_ref[...] = jnp.zeros(prev_ref.shape, F32)
    else:
        rseq = row % seq_rows
    acc = jnp.zeros(x.shape, F32)
    for ci in range(nchunk):
        c0 = ci * cw
        cs = slice(c0, c0 + cw)
        gt = _dot(h, wi_ref[:, cs])
        u = _dot(h, wi_ref[:, dff + c0:dff + c0 + cw])
        if carry_mode:
            p0, p1 = prev_ref[0:1, cs], prev_ref[1:2, cs]
            g1 = jnp.where(row == 0, p1, pltpu.roll(gt, 1, 0))
            g2 = jnp.where(row == 0, p0, jnp.where(row == 1, p1, pltpu.roll(gt, 2, 0)))
            last = gt[tm - 2:tm]
            prev_ref[:, cs] = last
            st_ref[0, :, cs] = last
        else:
            g1 = jnp.where(rseq >= 1, pltpu.roll(gt, 1, 0), f1_ref[:, cs])
            g2 = jnp.where(rseq >= 2, pltpu.roll(gt, 2, 0), f2_ref[:, cs])
            gate_ref[:, cs] = gt
        gc = cb_ref[:, cs] + g2 * cwt_ref[0:1, cs] + g1 * cwt_ref[1:2, cs] + gt * cwt_ref[2:3, cs]
        acc = acc + _dot((_silu(gc) * u).astype(BF16), wo_ref[cs, :])
    y = x + acc
    if final_norm:
        y = _rms(y, gf_ref[...])
    o_ref[...] = y


def _ffn(x, g, wi, cwt, cb, wo, gfinal, final_norm, seq_rows, tm, fills=None):
    m, d = x.shape
    dff = wo.shape[0]
    cw = 256
    carry_mode = fills is None
    nseq = m // seq_rows
    row = lambda width: pl.BlockSpec((tm, width), lambda i: (i, 0))
    ins = [x, g.reshape(1, d), wi, cwt, cb.reshape(1, dff), wo, gfinal.reshape(1, d)]
    in_specs = [row(d), _const_spec((1, d)), _const_spec(wi.shape), _const_spec(cwt.shape),
                _const_spec((1, dff)), _const_spec(wo.shape), _const_spec((1, d))]
    if carry_mode:
        per = seq_rows // tm
        out_specs = [row(d), pl.BlockSpec((1, 2, dff), lambda i: (i // per, 0, 0))]
        out_shape = [jax.ShapeDtypeStruct((m, d), F32), jax.ShapeDtypeStruct((nseq, 2, dff), F32)]
        scratch = [pltpu.VMEM((2, dff), F32)]
    else:
        per = 1
        ins += list(fills)
        in_specs += [row(dff), row(dff)]
        out_specs = [row(d), row(dff)]
        out_shape = [jax.ShapeDtypeStruct((m, d), F32), jax.ShapeDtypeStruct((m, dff), F32)]
        scratch = []
    return pl.pallas_call(
        functools.partial(_ffn_kernel, carry_mode=carry_mode, seq_rows=seq_rows, blocks_per_seq=per,
                          nchunk=dff // cw, cw=cw, final_norm=final_norm),
        grid=(m // tm,),
        in_specs=in_specs,
        out_specs=out_specs,
        out_shape=out_shape,
        scratch_shapes=scratch,
        compiler_params=_cp(("arbitrary",)),
        name="ffn",
    )(*ins)


def _sort_key(x):
    bits = lax.bitcast_convert_type(x, I32)
    key = jnp.where(bits < 0, bits ^ 0x7FFFFFFF, bits)
    return jnp.where(x == 0.0, 0, key)


def _fold_lanes(c):
    parts = [c[:, j * LANES:(j + 1) * LANES] for j in range(c.shape[1] // LANES)]
    while len(parts) > 1:
        parts = [parts[j] + parts[j + 1] for j in range(0, len(parts) - 1, 2)] + (parts[-1:] if len(parts) % 2 else [])
    return parts[0]


def _select_topk(keys_ref, n_tiles, topk, idx_bits):
    _, rows, tw = keys_ref.shape
    lane = lax.broadcasted_iota(I32, (rows, tw), 1)

    def count(ind):
        def body(j, acc):
            return acc + _fold_lanes(ind(keys_ref[j], j))
        acc = lax.fori_loop(0, n_tiles, body, jnp.zeros((rows, LANES), F32))
        return jnp.sum(acc, axis=-1, keepdims=True)

    def thr_step(it, thr):
        trial = thr ^ jnp.left_shift(jnp.int32(1), 31 - it)
        cnt = count(lambda kt, j: jnp.where(kt >= trial, 1.0, 0.0))
        return jnp.where(cnt >= topk, trial, thr)

    thr = lax.fori_loop(0, 32, thr_step, jnp.full((rows, 1), INT_MIN, I32))
    thr = jnp.maximum(thr, INT_MIN + 1)
    need = topk - count(lambda kt, j: jnp.where(kt > thr, 1.0, 0.0))
    ties = count(lambda kt, j: jnp.where(kt == thr, 1.0, 0.0))
    excess = jnp.max(ties - need) > 0.0

    def cut_step(it, cut):
        trial = cut | jnp.left_shift(jnp.int32(1), idx_bits - 1 - it)
        cnt = count(lambda kt, j: jnp.where(kt == thr, jnp.where(lane + j * tw < trial, 1.0, 0.0), 0.0))
        return jnp.where(cnt <= need, trial, cut)

    cut0 = jnp.full((rows, 1), jnp.where(excess, 0, 1 << idx_bits), I32)
    cut = lax.fori_loop(0, jnp.where(excess, idx_bits, 0), cut_step, cut0)
    return thr, cut


def _selection_bias(kt, j, tw, thr, cut):
    lane = lax.broadcasted_iota(I32, kt.shape, 1)
    return jnp.where(kt > thr, 0.0, jnp.where(kt == thr, jnp.where(lane + j * tw < cut, 0.0, NEG), NEG))


def _index_scores(iq, iw, ik2_tile, nhi, di):
    lane = lax.broadcasted_iota(I32, (iq.shape[0], LANES), 1)
    acc = None
    for pr in range(nhi * di // LANES):
        pair = iq[:, pr * LANES:(pr + 1) * LANES]
        zero = jnp.zeros_like(pair)
        for half_i, qm in enumerate((jnp.where(lane < di, pair, zero), jnp.where(lane >= di, pair, zero))):
            hd = 2 * pr + half_i
            term = jnp.maximum(_dot_nt(qm, ik2_tile), 0.0) * iw[:, hd:hd + 1]
            acc = term if acc is None else acc + term
    return acc


def _dsa_index_kernel(iq_ref, tail_ref, ik2_ref, bias_ref, keys_ref, *, tq, tw, topk, nhi, di, idx_bits):
    i = pl.program_id(1)
    nt_all = keys_ref.shape[0]
    n_tiles = (i * tq + tq + tw - 1) // tw
    iq = iq_ref[...]
    iw = tail_ref[:, di:di + nhi]
    qpos = i * tq + lax.broadcasted_iota(I32, (tq, tw), 0)
    lane = lax.broadcasted_iota(I32, (tq, tw), 1)

    def score_tile(j, c):
        r0 = pl.multiple_of(j * tw, tw)
        sc = _index_scores(iq, iw, ik2_ref[pl.ds(r0, tw), :], nhi, di)
        keys_ref[j] = jnp.where(lane + j * tw <= qpos, _sort_key(sc), INT_MIN)
        return c

    lax.fori_loop(0, n_tiles, score_tile, 0)
    thr, cut = _select_topk(keys_ref, n_tiles, topk, idx_bits)

    def write_tile(j, c):
        bias_ref[0, j] = _selection_bias(keys_ref[j], j, tw, thr, cut).astype(bias_ref.dtype)
        return c

    lax.fori_loop(0, n_tiles, write_tile, 0)

    def fill_tile(j, c):
        bias_ref[0, j] = jnp.full((tq, tw), NEG, bias_ref.dtype)
        return c

    lax.fori_loop(n_tiles, nt_all, fill_tile, 0)


def _dsa_index_prompt(iq, tail, ik2, b, t, topk, nhi, di, tq, tw):
    m = iq.shape[0]
    nq = t // tq
    nt = t // tw
    return pl.pallas_call(
        functools.partial(_dsa_index_kernel, tq=tq, tw=tw, topk=topk, nhi=nhi, di=di,
                          idx_bits=max(1, t.bit_length())),
        grid=(b, nq),
        in_specs=[pl.BlockSpec((tq, iq.shape[1]), lambda bb, i: (bb * nq + i, 0)),
                  pl.BlockSpec((tq, LANES), lambda bb, i: (bb * nq + i, 0)),
                  pl.BlockSpec((t, LANES), lambda bb, i: (bb, 0))],
        out_specs=pl.BlockSpec((1, nt, tq, tw), lambda bb, i: (bb * nq + i, 0, 0, 0)),
        out_shape=jax.ShapeDtypeStruct((m // tq, nt, tq, tw), BF16),
        scratch_shapes=[pltpu.VMEM((nt, tq, tw), I32)],
        compiler_params=_cp(("arbitrary", "arbitrary")),
        name="dsa_index",
    )(iq, tail, ik2)


def _gqa_update(g, s, vlist, m_ref, l_ref, a_ref):
    m_old = m_ref[g]
    m_new = jnp.maximum(m_old, jnp.max(s, axis=-1, keepdims=True))
    alpha = jnp.exp(m_old - m_new)
    p = jnp.exp(s - m_new[:, :1])
    l_ref[g] = alpha * l_ref[g] + jnp.sum(p, axis=-1, keepdims=True)
    pb = p.astype(BF16)
    kw = s.shape[1] // len(vlist)
    pv = _dot(pb[:, :kw], vlist[0])
    for r in range(1, len(vlist)):
        pv = pv + _dot(pb[:, r * kw:(r + 1) * kw], vlist[r])
    a_ref[g] = alpha * a_ref[g] + pv
    m_ref[g] = m_new


def _stack_heads(q, g, per, dh):
    return jnp.concatenate([q[:, (g * per + hl) * dh:(g * per + hl + 1) * dh] for hl in range(per)], axis=0)


def _unstack_heads(a_ref, l_ref, nkv, per, rows):
    outs = []
    for g in range(nkv):
        o = a_ref[g] / l_ref[g][:, :1]
        outs += [o[hl * rows:(hl + 1) * rows] for hl in range(per)]
    return jnp.concatenate(outs, axis=1)


def _dsa_attn_kernel(q_ref, k_ref, v_ref, bias_ref, o_ref, m_ref, l_ref, a_ref, *, tq, tw, nkv, per, dh):
    i = pl.program_id(1)
    n_tiles = (i * tq + tq + tw - 1) // tw
    m_ref[...] = jnp.full(m_ref.shape, NEG, F32)
    l_ref[...] = jnp.zeros(l_ref.shape, F32)
    a_ref[...] = jnp.zeros(a_ref.shape, F32)
    q = q_ref[...]
    qg = [_stack_heads(q, g, per, dh) for g in range(nkv)]

    def body(j, c):
        r0 = pl.multiple_of(j * tw, tw)
        bias = bias_ref[0, j].astype(F32)
        bias = jnp.concatenate([bias] * per, axis=0)
        for g in range(nkv):
            kb = k_ref[pl.ds(r0, tw), g * dh:(g + 1) * dh]
            vb = v_ref[pl.ds(r0, tw), g * dh:(g + 1) * dh]
            _gqa_update(g, _dot_nt(qg[g], kb) + bias, [vb], m_ref, l_ref, a_ref)
        return c

    lax.fori_loop(0, n_tiles, body, 0)
    o_ref[...] = _unstack_heads(a_ref, l_ref, nkv, per, tq).astype(o_ref.dtype)


def _dsa_attn_prompt(q, kb, vb, bias, b, t, nkv, tq, tw):
    m, w = q.shape
    dh = kb.shape[1] // nkv
    per = w // dh // nkv
    nq = t // tq
    nt = t // tw
    return pl.pallas_call(
        functools.partial(_dsa_attn_kernel, tq=tq, tw=tw, nkv=nkv, per=per, dh=dh),
        grid=(b, nq),
        in_specs=[pl.BlockSpec((tq, w), lambda bb, i: (bb * nq + i, 0)),
                  pl.BlockSpec((t, nkv * dh), lambda bb, i: (bb, 0)),
                  pl.BlockSpec((t, nkv * dh), lambda bb, i: (bb, 0)),
                  pl.BlockSpec((1, nt, tq, tw), lambda bb, i: (bb * nq + i, 0, 0, 0))],
        out_specs=pl.BlockSpec((tq, w), lambda bb, i: (bb * nq + i, 0)),
        out_shape=jax.ShapeDtypeStruct((m, w), BF16),
        scratch_shapes=[pltpu.VMEM((nkv, per * tq, LANES), F32), pltpu.VMEM((nkv, per * tq, LANES), F32),
                        pltpu.VMEM((nkv, per * tq, dh), F32)],
        compiler_params=_cp(("arbitrary", "arbitrary")),
        name="dsa_attn",
    )(q, kb, vb, bias)


def _dsa_dec_scores_kernel(pt_ref, iq_ref, iw_ref, ikn_ref, *rest, G, td, nhi, page, wnew):
    ikp = rest[:G]
    past_ref, new_ref = rest[G:]
    iq = iq_ref[0].astype(BF16)
    iw = iw_ref[0]

    def scores(ik):
        s = jnp.maximum(_dot_nt(iq, ik), 0.0) * iw
        out = s[0:td]
        for h in range(1, nhi):
            out = out + s[h * td:(h + 1) * td]
        return out

    past_ref[0] = jnp.concatenate([scores(ikp[r][0].astype(BF16)) for r in range(G)], axis=1)

    @pl.when(pl.program_id(1) == pl.num_programs(1) - 1)
    def _():
        sn = scores(_pad_rows(ikn_ref[...], wnew).astype(BF16))
        row = lax.broadcasted_iota(I32, sn.shape, 0)
        col = lax.broadcasted_iota(I32, sn.shape, 1)
        new_ref[0] = jnp.where(col <= row, sn, -jnp.inf)


def _dsa_dec_scores(pt, iq_hm, iw_hm, ik_new, pool_ik, base, bd, td, nhi, G, wnew):
    di = pool_ik.shape[2]
    page = pool_ik.shape[1]
    npg = pt.shape[1] // G
    seq3 = lambda shape: pl.BlockSpec((1,) + shape, lambda s, p, pt_: (s, 0, 0))
    grid_spec = pltpu.PrefetchScalarGridSpec(
        num_scalar_prefetch=1,
        grid=(bd, npg),
        in_specs=[seq3((nhi * td, di)), seq3((nhi * td, 1)), pl.BlockSpec((td, di), lambda s, p, pt_: (s, 0))]
                 + [pl.BlockSpec((1, page, di), functools.partial(lambda s, p, pt_, r: (base + pt_[s, p * G + r], 0, 0), r=r))
                    for r in range(G)],
        out_specs=[pl.BlockSpec((1, td, G * page), lambda s, p, pt_: (s, 0, p)), seq3((td, wnew))],
    )
    return pl.pallas_call(
        functools.partial(_dsa_dec_scores_kernel, G=G, td=td, nhi=nhi, page=page, wnew=wnew),
        grid_spec=grid_spec,
        out_shape=[jax.ShapeDtypeStruct((bd, td, pt.shape[1] * page), F32), jax.ShapeDtypeStruct((bd, td, wnew), F32)],
        compiler_params=_cp(("arbitrary", "arbitrary")),
        name="dsa_dec_scores",
    )(pt, iq_hm, iw_hm, ik_new, *([pool_ik] * G))


def _dsa_dec_select_kernel(sc_ref, bias_ref, keys_ref, *, topk, idx_bits):
    nt, rows, tw = keys_ref.shape
    for j in range(nt):
        sc = sc_ref[j]
        keys_ref[j] = jnp.where(sc == -jnp.inf, INT_MIN, _sort_key(sc))
    thr, cut = _select_topk(keys_ref, nt, topk, idx_bits)
    for j in range(nt):
        bias_ref[j] = _selection_bias(keys_ref[j], j, tw, thr, cut)


def _dsa_dec_select(sc_tiles, topk, rows):
    nt, m, tw = sc_tiles.shape
    spec = pl.BlockSpec((nt, rows, tw), lambda i: (0, i, 0))
    return pl.pallas_call(
        functools.partial(_dsa_dec_select_kernel, topk=topk, idx_bits=max(1, (nt * tw).bit_length())),
        grid=(m // rows,),
        in_specs=[spec],
        out_specs=spec,
        out_shape=jax.ShapeDtypeStruct((nt, m, tw), F32),
        scratch_shapes=[pltpu.VMEM((nt, rows, tw), I32)],
        compiler_params=_cp(("arbitrary",)),
        name="dsa_dec_select",
    )(sc_tiles)


def _dsa_dec_attn_kernel(pt_ref, q_ref, kn_ref, vn_ref, bias_ref, bnew_ref, *rest, G, td, nkv, per, dh, page):
    kp, vp = rest[:G], rest[G:2 * G]
    o_ref, m_ref, l_ref, a_ref = rest[2 * G:]
    p = pl.program_id(1)
    tw = bias_ref.shape[2]

    @pl.when(p == 0)
    def _():
        m_ref[...] = jnp.full(m_ref.shape, NEG, F32)
        l_ref[...] = jnp.zeros(l_ref.shape, F32)
        a_ref[...] = jnp.zeros(a_ref.shape, F32)

    q = q_ref[...].astype(BF16)
    qg = [_stack_heads(q, g, per, dh) for g in range(nkv)]
    bias = jnp.concatenate([bias_ref[j] for j in range(bias_ref.shape[0])], axis=1)
    bias = jnp.concatenate([bias] * per, axis=0)
    for g in range(nkv):
        s = jnp.concatenate([_dot_nt(qg[g], kp[r][:, g, :].astype(BF16)) for r in range(G)], axis=1)
        _gqa_update(g, s + bias, [vp[r][:, g, :].astype(BF16) for r in range(G)], m_ref, l_ref, a_ref)

    @pl.when(p == pl.num_programs(1) - 1)
    def _():
        bias_n = jnp.concatenate([bnew_ref[0, :, 0:page]] * per, axis=0)
        for g in range(nkv):
            kn = _pad_rows(kn_ref[:, g, :], page).astype(BF16)
            vn = _pad_rows(vn_ref[:, g, :], page).astype(BF16)
            _gqa_update(g, _dot_nt(qg[g], kn) + bias_n, [vn], m_ref, l_ref, a_ref)
        o_ref[...] = _unstack_heads(a_ref, l_ref, nkv, per, td)


def _dsa_dec_attn(pt, q, kn, vn, bias_tiles, pool_k, pool_v, base, page, bd, td, nkv, G):
    m, w = q.shape
    dh = pool_k.shape[2]
    per = w // dh // nkv
    npg = pt.shape[1] // G
    nt, _, tw = bias_tiles.shape
    tiles_per_step = G * page // tw
    seq = lambda width: pl.BlockSpec((td, width), lambda s, p, pt_: (s, 0))
    new = pl.BlockSpec((td, nkv, dh), lambda s, p, pt_: (s, 0, 0))

    def page_spec(r):
        return pl.BlockSpec((page, nkv, dh), lambda s, p, pt_: (base + pt_[s, p * G + r], 0, 0))

    grid_spec = pltpu.PrefetchScalarGridSpec(
        num_scalar_prefetch=1,
        grid=(bd, npg),
        in_specs=[seq(w), new, new,
                  pl.BlockSpec((tiles_per_step, td, tw), lambda s, p, pt_: (p, s, 0)),
                  pl.BlockSpec((1, td, tw), lambda s, p, pt_: (nt - 1, s, 0))]
                 + [page_spec(r) for r in range(G)] * 2,
        out_specs=seq(w),
        scratch_shapes=[pltpu.VMEM((nkv, per * td, LANES), F32), pltpu.VMEM((nkv, per * td, LANES), F32),
                        pltpu.VMEM((nkv, per * td, dh), F32)],
    )
    return pl.pallas_call(
        functools.partial(_dsa_dec_attn_kernel, G=G, td=td, nkv=nkv, per=per, dh=dh, page=page),
        grid_spec=grid_spec,
        out_shape=jax.ShapeDtypeStruct((m, w), F32),
        compiler_params=_cp(("arbitrary", "arbitrary")),
        name="dsa_dec_attn",
    )(pt, q, kn, vn, bias_tiles, bias_tiles, *([pool_k] * G), *([pool_v] * G))


def kernel(x_prompt, x_sample, cache_diff_k, cache_diff_v, state_hgrn, cache_dsa_k, cache_dsa_v, cache_dsa_ik,
           cache_mem_k, cache_mem_v, state_ffn_conv, page_table, mem_prompt, norm_mix, w_in_ab, diff_lq1, diff_lk1,
           diff_lq2, diff_lk2, diff_subln, hgrn_lb_logits, hgrn_onorm, w_out_ab, w_in_c, idx_k_norm, w_out_c,
           norm_x, norm_mem, w_xq, w_xk, w_xv, w_xo, norm_ffn, w_ffn_in, conv_ffn_w, conv_ffn_b, w_ffn_out,
           norm_final):
    b, t, d = x_prompt.shape
    bd, td, _ = x_sample.shape
    depth = norm_mix.shape[0]
    n_ab = w_in_ab.shape[0]
    ha, dva = cache_diff_v.shape[3], cache_diff_v.shape[4]
    dqa = cache_diff_k.shape[4] // 2
    hb, dkb = state_hgrn.shape[2], state_hgrn.shape[3]
    kvc, dhc = cache_dsa_k.shape[3], cache_dsa_k.shape[4]
    hc = d // dhc
    di = cache_dsa_ik.shape[3]
    hi = w_in_c.shape[2] - (hc + 2 * kvc) * dhc - di
    hi = hi // (di + 1)
    hx = cache_mem_k.shape[3]
    n_mem = mem_prompt.shape[1]
    dff = w_ffn_out.shape[1]
    n_pool, page = cache_diff_k.shape[1], cache_diff_k.shape[2]
    past = page_table.shape[1] * page
    mp, ms = b * t, bd * td
    G = 8

    pos_p = jnp.tile(jnp.arange(t), b)
    pos_s = past + jnp.tile(jnp.arange(td), bd)
    tab64_p, half64 = _rope_tables(pos_p, dqa)
    tab64_s, _ = _rope_tables(pos_s, dqa)
    tab128_p, half128 = _rope_tables(pos_p, dhc)
    tab128_s, _ = _rope_tables(pos_s, dhc)

    hgrn_lb = jnp.cumsum(jax.nn.softmax(hgrn_lb_logits.astype(F32), axis=0), axis=0)[:n_ab]
    xp = x_prompt.reshape(mp, d)
    xs = x_sample.reshape(ms, d)
    gfin = norm_final

    mkf, mvf, mkb, mvb = _memproj(mem_prompt.reshape(b * n_mem, d), norm_mem, w_xk.astype(BF16), w_xv.astype(BF16), 256)

    new_p = {k: [] for k in ('diff_k', 'diff_v', 'hgrn', 'dsa_k', 'dsa_v', 'dsa_ik', 'conv')}
    new_s = {k: [] for k in new_p}

    for l in range(depth):
        j = l // 2
        wq_b, wo_b = w_xq[l].astype(BF16), w_xo[l].astype(BF16)
        if l % 2 == 0:
            w_in = w_in_ab[j].astype(BF16)
            w_out = w_out_ab[j].astype(BF16)
            wa, wb = w_out[:ha * dva], w_out[ha * dva:]
            lam_init = 0.8 - 0.6 * math.exp(-0.3 * l)
            lam = (jnp.exp(jnp.sum(diff_lq1[j].astype(F32) * diff_lk1[j].astype(F32)))
                   - jnp.exp(jnp.sum(diff_lq2[j].astype(F32) * diff_lk2[j].astype(F32))) + lam_init)
            q, kf, vf, kb, vb, qh, kk, lf, ib, gb = _inproj_ab(
                xp, norm_mix[l], w_in, hgrn_lb[j], tab64_p, half64, dqa, dkb, BF16, 256)
            o_a = _diffattn_prompt(lam, q, kb, vb, diff_subln[j], b, t, ha, dqa, 1.0 - lam_init, 256)
            o_b, s_new = _hgrn(qh, kk, ib, lf, gb, hgrn_onorm[j], None, 0, b, t, hb, 512, 128, 8, BF16)
            new_p['diff_k'].append(kf.reshape(1, b, t, ha, 2 * dqa))
            new_p['diff_v'].append(vf.reshape(1, b, t, ha, dva))
            new_p['hgrn'].append(s_new[None])
            parts_p = [(o_a, wa), (o_b, wb)]
            q, kf, vf, kb, vb, qh, kk, lf, ib, gb = _inproj_ab(
                xs, norm_mix[l], w_in, hgrn_lb[j], tab64_s, half64, dqa, dkb, F32, ms)
            pool_k = cache_diff_k.reshape(-1, ha, 2 * dqa)
            pool_v = cache_diff_v.reshape(-1, ha, dva)
            o_a = _diff_decode(page_table, lam, q, kf, vf, diff_subln[j], pool_k, pool_v, j * n_pool, page, bd, td,
                               ha, dqa, 1.0 - lam_init, G)
            c_s = math.gcd(td, 64)
            o_b, s_new = _hgrn(qh, kk, ib, lf, gb, hgrn_onorm[j], state_hgrn.reshape(-1, hb, dkb, dkb), j * bd,
                               bd, td, hb, td, c_s, min(8, c_s), F32)
            new_s['diff_k'].append(kf.reshape(1, bd, td, ha, 2 * dqa))
            new_s['diff_v'].append(vf.reshape(1, bd, td, ha, dva))
            new_s['hgrn'].append(s_new[None])
            parts_s = [(o_a, wa), (o_b, wb)]
        else:
            w_in = w_in_c[j].astype(BF16)
            cw = w_in.shape[1]
            main = (hc + 2 * kvc) * dhc + hi * di
            w_pad = jnp.concatenate([w_in, jnp.zeros((d, main + LANES - cw), BF16)], axis=1)
            ikg_pad = jnp.concatenate([idx_k_norm[j].astype(F32), jnp.zeros((LANES - di,), F32)]).reshape(1, LANES)
            dims = (hc * dhc, kvc * dhc, hi * di, di, hi, dhc)
            w_out = w_out_c[j].astype(BF16)
            q, kf, vf, kb, vb, iq, tail, ik2 = _inproj_c(
                xp, norm_mix[l], w_pad, ikg_pad, tab128_p, tab64_p, half128, half64, dims, BF16, 256)
            topk = min(DSA_TOPK_MAX, t // 4)
            bias = _dsa_index_prompt(iq, tail, ik2, b, t, topk, hi, di, 128, 512)
            o_c = _dsa_attn_prompt(q, kb, vb, bias, b, t, kvc, 128, 512)
            new_p['dsa_k'].append(kf.reshape(1, b, t, kvc, dhc))
            new_p['dsa_v'].append(vf.reshape(1, b, t, kvc, dhc))
            new_p['dsa_ik'].append(tail[:, :di].reshape(1, b, t, di))
            parts_p = [(o_c, w_out)]
            q, kf, vf, kb, vb, iq, tail, ik2 = _inproj_c(
                xs, norm_mix[l], w_pad, ikg_pad, tab128_s, tab64_s, half128, half64, dims, F32, ms)
            tw = 512
            iq_hm = iq.reshape(bd, td, hi, di).transpose(0, 2, 1, 3).reshape(bd, hi * td, di)
            iw_hm = tail[:, di:di + hi].reshape(bd, td, hi).transpose(0, 2, 1).reshape(bd, hi * td, 1)
            ik_new = tail[:, :di]
            sc_past, sc_new = _dsa_dec_scores(page_table, iq_hm, iw_hm, ik_new, cache_dsa_ik.reshape(-1, page, di),
                                              j * n_pool, bd, td, hi, G, tw)
            sc = jnp.concatenate([sc_past, sc_new], axis=2).reshape(ms, -1)
            nt = sc.shape[1] // tw
            sc_tiles = sc.reshape(ms, nt, tw).transpose(1, 0, 2)
            topk = min(DSA_TOPK_MAX, (past + td) // 4)
            bias_tiles = _dsa_dec_select(sc_tiles, topk, 128 if ms % 128 == 0 else ms)
            pool_k = cache_dsa_k.reshape(-1, kvc, dhc)
            pool_v = cache_dsa_v.reshape(-1, kvc, dhc)
            o_c = _dsa_dec_attn(page_table, q, kf, vf, bias_tiles, pool_k, pool_v, j * n_pool, page, bd, td, kvc, G)
            new_s['dsa_k'].append(kf.reshape(1, bd, td, kvc, dhc))
            new_s['dsa_v'].append(vf.reshape(1, bd, td, kvc, dhc))
            new_s['dsa_ik'].append(ik_new.reshape(1, bd, td, di))
            parts_s = [(o_c, w_out)]

        last = l == depth - 1
        wi_b, wo2_b = w_ffn_in[l].astype(BF16), w_ffn_out[l].astype(BF16)
        xp = _cross_prompt(xp, parts_p, norm_x[l], wq_b, wo_b, mkb[l].reshape(b, n_mem, d), mvb[l].reshape(b, n_mem, d),
                           hx, t, 256)
        xp, conv_p = _ffn(xp, norm_ffn[l], wi_b, conv_ffn_w[l], conv_ffn_b[l], wo2_b, gfin, last, t, 256)
        new_p['conv'].append(conv_p[None])
        xs = _cross_sample(xs, parts_s, norm_x[l], wq_b, wo_b, cache_mem_k.reshape(-1, hx, d // hx),
                           cache_mem_v.reshape(-1, hx, d // hx), l * bd, n_mem, hx, bd, td)
        st = state_ffn_conv[l]
        zeros = jnp.zeros((bd, td - 1, dff), F32)
        fill1 = jnp.concatenate([st[:, 1:2], zeros], axis=1).reshape(ms, dff)
        fill2 = jnp.concatenate([st, zeros[:, 1:]], axis=1).reshape(ms, dff)
        xs, gate_s = _ffn(xs, norm_ffn[l], wi_b, conv_ffn_w[l], conv_ffn_b[l], wo2_b, gfin, last, td, ms,
                          fills=(fill1, fill2))
        new_s['conv'].append(gate_s.reshape(bd, td, dff)[None, :, td - 2:])

    join = lambda v: v[0] if len(v) == 1 else jnp.concatenate(v, axis=0)
    sp = {k: join(v) for k, v in new_p.items()}
    ss = {k: join(v) for k, v in new_s.items()}
    mk_p = mkf.reshape(depth, b, n_mem, hx, d // hx)
    mv_p = mvf.reshape(depth, b, n_mem, hx, d // hx)
    return (xp.reshape(b, t, d), xs.reshape(bd, td, d),
            sp['diff_k'], sp['diff_v'], sp['hgrn'], sp['dsa_k'], sp['dsa_v'], sp['dsa_ik'],
            mk_p, mv_p, sp['conv'],
            ss['diff_k'], ss['diff_v'], ss['hgrn'], ss['dsa_k'], ss['dsa_v'], ss['dsa_ik'], ss['conv'])
```

```python
import functools
import math

import jax
import jax.numpy as jnp
import numpy as np
from jax import lax
from jax.experimental import pallas as pl
from jax.experimental.pallas import tpu as pltpu

F32 = jnp.float32
BF16 = jnp.bfloat16
I32 = jnp.int32

EPS = 1e-6
ROPE_THETA = 500000.0
ROT_DIV = 4
DSA_TOPK_MAX = 256
NEG = -1e30
INT_MIN = -2147483648
LANES = 128
VMEM_LIMIT = 56 * 1024 * 1024

NT = (((1,), (1,)), ((), ()))


def _cp(sem):
    return pltpu.CompilerParams(dimension_semantics=sem, vmem_limit_bytes=VMEM_LIMIT)


def _dot(a, b):
    return jnp.dot(a, b, preferred_element_type=F32)


def _dot_nt(a, b):
    return lax.dot_general(a, b, NT, preferred_element_type=F32)


def _rms(x, g):
    ms = jnp.mean(x * x, axis=-1, keepdims=True)
    return x * lax.rsqrt(ms + EPS) * g


def _silu(x):
    return x * jax.nn.sigmoid(x)


def _tile_lanes(t, n):
    return t if n == 1 else jnp.concatenate([t] * n, axis=1)


def _rope(z, c, s1, s2, half):
    w = z.shape[1]
    n = w // LANES
    return (z * _tile_lanes(c, n) + pltpu.roll(z, half, 1) * _tile_lanes(s1, n)
            + pltpu.roll(z, w - half, 1) * _tile_lanes(s2, n))


def _rope_tables(pos, head_dim):
    half = head_dim // ROT_DIV // 2
    inv = jnp.exp(jnp.arange(half, dtype=F32) * (-math.log(ROPE_THETA) / half))
    ang = pos.astype(F32)[:, None] * inv[None, :]
    cos, sin = jnp.cos(ang), jnp.sin(ang)
    m = pos.shape[0]
    rest = head_dim - 2 * half
    c = jnp.concatenate([cos, cos, jnp.ones((m, rest), F32)], axis=1)
    s1 = jnp.concatenate([jnp.zeros((m, half), F32), sin, jnp.zeros((m, rest), F32)], axis=1)
    s2 = jnp.concatenate([-sin, jnp.zeros((m, half + rest), F32)], axis=1)
    rep = LANES // head_dim
    return tuple(jnp.tile(t, (1, rep)) for t in (c, s1, s2)), half


def _store_heads(ref, x):
    dim = ref.shape[2]
    for h in range(ref.shape[1]):
        ref[:, h, :] = x[:, h * dim:(h + 1) * dim]


def _const_spec(shape):
    nd = len(shape)
    return pl.BlockSpec(shape, lambda *_: (0,) * nd)


def _inproj_ab_kernel(x_ref, g_ref, w_ref, lb_ref, c_ref, s1_ref, s2_ref,
                      q_ref, kf_ref, vf_ref, kb_ref, vb_ref, qh_ref, kk_ref, lf_ref, ib_ref, gb_ref,
                      *, half, qscale, hscale):
    h = _rms(x_ref[...], g_ref[...]).astype(BF16)
    sw = q_ref.shape[1]

    def seg(i):
        return _dot(h, w_ref[:, i * sw:(i + 1) * sw])

    c, s1, s2 = c_ref[...], s1_ref[...], s2_ref[...]
    q_ref[...] = (_rope(seg(0), c, s1, s2, half) * qscale).astype(q_ref.dtype)
    ka = _rope(seg(1), c, s1, s2, half)
    _store_heads(kf_ref, ka)
    kb_ref[...] = ka.astype(BF16)
    va = seg(2)
    _store_heads(vf_ref, va)
    vb_ref[...] = va.astype(BF16)
    qh_ref[...] = _silu(seg(3)) * hscale
    lb = lb_ref[...]
    fg = lb + (1.0 - lb) * jax.nn.sigmoid(seg(4))
    kk_ref[...] = 1.0 - fg
    lf_ref[...] = jnp.log(fg)
    ib_ref[...] = seg(5)
    gb_ref[...] = _silu(seg(6))


def _inproj_ab(x, g, w, lb, tabs, half, dqa, dkb, qdtype, tm):
    m, d = x.shape
    sw = w.shape[1] // 7
    nh = sw // (2 * dqa)
    row = lambda width: pl.BlockSpec((tm, width), lambda i: (i, 0))
    heads = pl.BlockSpec((tm, nh, 2 * dqa), lambda i: (i, 0, 0))
    outs = [(sw, qdtype), (sw, F32), (sw, F32), (sw, BF16), (sw, BF16)] + [(sw, F32)] * 5
    out_specs = [row(wd) for wd, _ in outs]
    out_shape = [jax.ShapeDtypeStruct((m, wd), dt) for wd, dt in outs]
    for i in (1, 2):
        out_specs[i] = heads
        out_shape[i] = jax.ShapeDtypeStruct((m, nh, 2 * dqa), F32)
    return pl.pallas_call(
        functools.partial(_inproj_ab_kernel, half=half, qscale=dqa ** -0.5, hscale=dkb ** -0.5),
        grid=(m // tm,),
        in_specs=[row(d), _const_spec((1, d)), _const_spec(w.shape), _const_spec((1, sw)),
                  row(LANES), row(LANES), row(LANES)],
        out_specs=out_specs,
        out_shape=out_shape,
        compiler_params=_cp(("arbitrary",)),
        name="inproj_ab",
    )(x, g.reshape(1, d), w, lb.reshape(1, sw), *tabs)


def _inproj_c_kernel(x_ref, g_ref, w_ref, ikg_ref, c128_ref, s1128_ref, s2128_ref, c64_ref, s164_ref, s264_ref,
                     q_ref, kf_ref, vf_ref, kb_ref, vb_ref, iq_ref, tail_ref, ik2_ref,
                     *, nq, nkv, niq, di, half128, half64, qscale, iwscale):
    h = _rms(x_ref[...], g_ref[...]).astype(BF16)
    c1, a1, b1 = c128_ref[...], s1128_ref[...], s2128_ref[...]
    c6, a6, b6 = c64_ref[...], s164_ref[...], s264_ref[...]
    o = 0
    q = _dot(h, w_ref[:, o:o + nq]); o += nq
    q_ref[...] = (_rope(q, c1, a1, b1, half128) * qscale).astype(q_ref.dtype)
    k = _rope(_dot(h, w_ref[:, o:o + nkv]), c1, a1, b1, half128); o += nkv
    _store_heads(kf_ref, k)
    kb_ref[...] = k.astype(BF16)
    v = _dot(h, w_ref[:, o:o + nkv]); o += nkv
    _store_heads(vf_ref, v)
    vb_ref[...] = v.astype(BF16)
    iq = _rope(_dot(h, w_ref[:, o:o + niq]), c6, a6, b6, half64); o += niq
    iq_ref[...] = iq.astype(iq_ref.dtype)
    t = _dot(h, w_ref[:, o:o + LANES])
    lane = lax.broadcasted_iota(I32, t.shape, 1)
    is_k = lane < di
    tk = jnp.where(is_k, t, 0.0)
    ms = jnp.sum(tk * tk, axis=-1, keepdims=True) * (1.0 / di)
    ikn = tk * lax.rsqrt(ms + EPS) * ikg_ref[...]
    ikr = _rope(ikn, c6, a6, b6, half64)
    ikr = jnp.where(is_k, ikr, 0.0)
    tail_ref[...] = jnp.where(is_k, ikr, t * iwscale)
    ik2_ref[...] = (ikr + pltpu.roll(ikr, di, 1)).astype(BF16)


def _inproj_c(x, g, w_pad, ikg_pad, tabs128, tabs64, half128, half64, dims, qdtype, tm):
    m, d = x.shape
    nq, nkv, niq, di, hi, dhc = dims
    row = lambda width: pl.BlockSpec((tm, width), lambda i: (i, 0))
    outs = [(nq, qdtype), (nkv, F32), (nkv, F32), (nkv, BF16), (nkv, BF16), (niq, qdtype), (LANES, F32), (LANES, BF16)]
    out_specs = [row(wd) for wd, _ in outs]
    out_shape = [jax.ShapeDtypeStruct((m, wd), dt) for wd, dt in outs]
    for i in (1, 2):
        out_specs[i] = pl.BlockSpec((tm, nkv // dhc, dhc), lambda i_: (i_, 0, 0))
        out_shape[i] = jax.ShapeDtypeStruct((m, nkv // dhc, dhc), F32)
    return pl.pallas_call(
        functools.partial(_inproj_c_kernel, nq=nq, nkv=nkv, niq=niq, di=di, half128=half128, half64=half64,
                          qscale=dhc ** -0.5, iwscale=(hi * di) ** -0.5),
        grid=(m // tm,),
        in_specs=[row(d), _const_spec((1, d)), _const_spec(w_pad.shape), _const_spec((1, LANES))]
                 + [row(LANES)] * 6,
        out_specs=out_specs,
        out_shape=out_shape,
        compiler_params=_cp(("arbitrary",)),
        name="inproj_c",
    )(x, g.reshape(1, d), w_pad, ikg_pad, *tabs128, *tabs64)


def _memproj_kernel(x_ref, g_ref, wk_ref, wv_ref, kf_ref, vf_ref, kb_ref, vb_ref):
    h = _rms(x_ref[...], g_ref[0]).astype(BF16)
    k = _dot(h, wk_ref[0])
    v = _dot(h, wv_ref[0])
    kf_ref[0] = k
    vf_ref[0] = v
    kb_ref[0] = k.astype(BF16)
    vb_ref[0] = v.astype(BF16)


def _memproj(x, g, wk, wv, tm):
    m, d = x.shape
    depth, _, n = wk.shape
    wspec = pl.BlockSpec((1, d, n), lambda l, i: (l, 0, 0))
    ospec = pl.BlockSpec((1, tm, n), lambda l, i: (l, i, 0))
    return pl.pallas_call(
        _memproj_kernel,
        grid=(depth, m // tm),
        in_specs=[pl.BlockSpec((tm, d), lambda l, i: (i, 0)), pl.BlockSpec((1, 1, d), lambda l, i: (l, 0, 0)),
                  wspec, wspec],
        out_specs=[ospec] * 4,
        out_shape=[jax.ShapeDtypeStruct((depth, m, n), dt) for dt in (F32, F32, BF16, BF16)],
        compiler_params=_cp(("arbitrary", "arbitrary")),
        name="memproj",
    )(x, g.reshape(depth, 1, d), wk, wv)


def _softmax_step(s, vb, m, l, a):
    m_new = jnp.maximum(m, jnp.max(s, axis=-1, keepdims=True))
    alpha = jnp.exp(m - m_new)
    p = jnp.exp(s - m_new)
    l = alpha * l + jnp.sum(p, axis=-1, keepdims=True)
    a = alpha * a + _dot(p.astype(BF16), vb)
    return m_new, l, a


def _diff_finish(a1, l1, a2, l2, lam, g, out_scale):
    o = a1 / l1 - lam * (a2 / l2)
    return _rms(o, g) * out_scale


def _diffattn_kernel(lam_ref, q_ref, k_ref, v_ref, g_ref, o_ref, *, tq, dq, out_scale):
    i = pl.program_id(2)
    q = q_ref[...]
    lane = lax.broadcasted_iota(I32, q.shape, 1)
    zero = jnp.zeros_like(q)
    q1 = jnp.where(lane < dq, q, zero)
    q2 = jnp.where(lane >= dq, q, zero)
    dv = v_ref.shape[1]
    causal = (lax.broadcasted_iota(I32, (tq, tq), 0) >= lax.broadcasted_iota(I32, (tq, tq), 1))
    causal_bias = jnp.where(causal, 0.0, NEG)

    def scores(j):
        kb = k_ref[pl.ds(pl.multiple_of(j * tq, tq), tq), :]
        bias = causal_bias * (j == i).astype(F32)
        return _dot_nt(q1, kb) + bias, _dot_nt(q2, kb) + bias

    def attend(j, s1, s2, state):
        vb = v_ref[pl.ds(pl.multiple_of(j * tq, tq), tq), :]
        m1, l1, a1, m2, l2, a2 = state
        return _softmax_step(s1, vb, m1, l1, a1) + _softmax_step(s2, vb, m2, l2, a2)

    def init():
        return (jnp.full((tq, 1), NEG, F32), jnp.zeros((tq, 1), F32), jnp.zeros((tq, dv), F32))

    def body(j, carry):
        s1, s2, state = carry
        n1, n2 = scores(j + 1)
        return n1, n2, attend(j, s1, s2, state)

    s1, s2, state = lax.fori_loop(0, i, body, scores(0) + (init() + init(),))
    m1, l1, a1, m2, l2, a2 = attend(i, s1, s2, state)
    o_ref[...] = _diff_finish(a1, l1, a2, l2, lam_ref[0], g_ref[...], out_scale).astype(o_ref.dtype)


def _diffattn_prompt(lam, q, kb, vb, g, b, t, nh, dq, out_scale, tq):
    m, w = q.shape
    dv = w // nh
    nq = t // tq
    smem = pl.BlockSpec(memory_space=pltpu.SMEM)
    return pl.pallas_call(
        functools.partial(_diffattn_kernel, tq=tq, dq=dq, out_scale=out_scale),
        grid=(b, nh, nq),
        in_specs=[smem,
                  pl.BlockSpec((tq, dv), lambda bb, h, i: (bb * nq + i, h)),
                  pl.BlockSpec((t, dv), lambda bb, h, i: (bb, h)),
                  pl.BlockSpec((t, dv), lambda bb, h, i: (bb, h)),
                  _const_spec((1, dv))],
        out_specs=pl.BlockSpec((tq, dv), lambda bb, h, i: (bb * nq + i, h)),
        out_shape=jax.ShapeDtypeStruct((m, w), BF16),
        compiler_params=_cp(("arbitrary",) * 3),
        name="diffattn_prompt",
    )(lam.reshape(1), q, kb, vb, g.reshape(1, dv))


def _pad_rows(x, rows):
    return jnp.concatenate([x, jnp.zeros((rows - x.shape[0], x.shape[1]), x.dtype)], axis=0)


def _head_mask_bias(rows, cols, nh, rows_per_head):
    r = lax.broadcasted_iota(I32, (rows, cols), 0)
    c = lax.broadcasted_iota(I32, (rows, cols), 1)
    own = jnp.bitwise_and(c, nh - 1) == lax.shift_right_logical(r, int(math.log2(rows_per_head)))
    return jnp.where(own, 0.0, NEG)


def _online_update(s, vlist, m_ref, l_ref, a_ref):
    m_old = m_ref[...]
    m_new = jnp.maximum(m_old, jnp.max(s, axis=-1, keepdims=True))
    alpha = jnp.exp(m_old - m_new)
    p = jnp.exp(s - m_new[:, :1]).astype(BF16)
    l_ref[...] = alpha * l_ref[...] + jnp.sum(p.astype(F32), axis=-1, keepdims=True)
    kw = s.shape[1] // len(vlist)
    pv = _dot(p[:, :kw], vlist[0])
    for r in range(1, len(vlist)):
        pv = pv + _dot(p[:, r * kw:(r + 1) * kw], vlist[r])
    a_ref[...] = alpha * a_ref[...] + pv
    m_ref[...] = m_new


def _init_softmax_state(m_ref, l_ref, a_ref):
    m_ref[...] = jnp.full(m_ref.shape, NEG, F32)
    l_ref[...] = jnp.zeros(l_ref.shape, F32)
    a_ref[...] = jnp.zeros(a_ref.shape, F32)


def _diff_decode_kernel(pt_ref, lam_ref, q_ref, kn_ref, vn_ref, g_ref, *rest, G, nh, dq, out_scale):
    kp, vp = rest[:G], rest[G:2 * G]
    o_ref, m_ref, l_ref, a_ref, hm_ref = rest[2 * G:]
    p = pl.program_id(1)
    td = q_ref.shape[0]
    dv = q_ref.shape[1] // nh
    rows = nh * 2 * td

    @pl.when(p == 0)
    def _():
        _init_softmax_state(m_ref, l_ref, a_ref)
        hm_ref[...] = _head_mask_bias(rows, hm_ref.shape[1], nh, 2 * td)

    q = q_ref[...].astype(BF16)
    lane = lax.broadcasted_iota(I32, (td, dv), 1)
    parts = []
    for h in range(nh):
        qh = q[:, h * dv:(h + 1) * dv]
        zero = jnp.zeros_like(qh)
        parts += [jnp.where(lane < dq, qh, zero), jnp.where(lane >= dq, qh, zero)]
    qall = jnp.concatenate(parts, axis=0)
    s = jnp.concatenate([_dot_nt(qall, kp[r][...].astype(BF16)) for r in range(G)], axis=1) + hm_ref[...]
    _online_update(s, [vp[r][...].astype(BF16) for r in range(G)], m_ref, l_ref, a_ref)

    @pl.when(p == pl.num_programs(1) - 1)
    def _():
        kn = _pad_rows(kn_ref[...], LANES).astype(BF16)
        vn = _pad_rows(vn_ref[...], LANES).astype(BF16)
        r = lax.broadcasted_iota(I32, (rows, LANES), 0)
        c = lax.broadcasted_iota(I32, (rows, LANES), 1)
        own = jnp.bitwise_and(c, nh - 1) == lax.shift_right_logical(r, int(math.log2(2 * td)))
        seen = lax.shift_right_logical(c, int(math.log2(nh))) <= jnp.bitwise_and(r, td - 1)
        sn = jnp.where(own, jnp.where(seen, _dot_nt(qall, kn), NEG), NEG)
        _online_update(sn, [vn], m_ref, l_ref, a_ref)
        a, l = a_ref[...], l_ref[...]
        outs = []
        for h in range(nh):
            lo = h * 2 * td
            outs.append(_diff_finish(a[lo:lo + td], l[lo:lo + td], a[lo + td:lo + 2 * td], l[lo + td:lo + 2 * td],
                                     lam_ref[0], g_ref[...], out_scale))
        o_ref[...] = jnp.concatenate(outs, axis=1)


def _diff_decode(pt, lam, q, kn, vn, g, pool_k, pool_v, base, page, bd, td, nh, dq, out_scale, G):
    m, w = q.shape
    dv = w // nh
    assert dv == LANES and nh & (nh - 1) == 0 and td & (td - 1) == 0 and td * nh <= LANES
    npg = pt.shape[1] // G
    rows = nh * 2 * td
    seq = pl.BlockSpec((td, w), lambda s, p, pt_: (s, 0))
    new = pl.BlockSpec((td * nh, dv), lambda s, p, pt_: (s, 0))

    def page_spec(r):
        return pl.BlockSpec((page * nh, dv), lambda s, p, pt_: (base + pt_[s, p * G + r], 0))

    grid_spec = pltpu.PrefetchScalarGridSpec(
        num_scalar_prefetch=1,
        grid=(bd, npg),
        in_specs=[pl.BlockSpec(memory_space=pltpu.SMEM), seq, new, new,
                  pl.BlockSpec((1, dv), lambda s, p, pt_: (0, 0))]
                 + [page_spec(r) for r in range(G)] * 2,
        out_specs=seq,
        scratch_shapes=[pltpu.VMEM((rows, LANES), F32), pltpu.VMEM((rows, LANES), F32), pltpu.VMEM((rows, dv), F32),
                        pltpu.VMEM((rows, G * page * nh), F32)],
    )
    return pl.pallas_call(
        functools.partial(_diff_decode_kernel, G=G, nh=nh, dq=dq, out_scale=out_scale),
        grid_spec=grid_spec,
        out_shape=jax.ShapeDtypeStruct((m, w), F32),
        compiler_params=_cp(("arbitrary", "arbitrary")),
        name="diffattn_decode",
    )(pt, lam.reshape(1), q, kn, vn, g.reshape(1, dv), *([pool_k] * G), *([pool_v] * G))


def _cumsum_rows(x):
    n = x.shape[0]
    row = lax.broadcasted_iota(I32, x.shape, 0)
    s = 1
    while s < n:
        x = x + jnp.where(row >= s, pltpu.roll(x, s, 0), 0.0)
        s *= 2
    return x


def _block_rows(b, blk, edge, keep):
    n = b.shape[0]
    parts = []
    for p in range(n // blk):
        lo = p * blk
        if keep(p):
            r = lo - 1 if edge == 'prev_end' else lo + blk - 1
            parts.append(jnp.broadcast_to(b[r:r + 1], (blk, b.shape[1])))
        else:
            parts.append(b[lo:lo + blk])
    return jnp.concatenate(parts, axis=0)


def _hgrn_chunk(q, k, v, lf, st, sub):
    c, dk = q.shape
    b = _cumsum_rows(lf)
    o = _dot_nt((q * jnp.exp(b)).astype(BF16), st.astype(BF16))
    row_s = lax.broadcasted_iota(I32, (sub, dk), 0)
    parts = []
    for i in range(c // sub):
        lo = i * sub
        qi, ki, vi, bi = (a[lo:lo + sub] for a in (q, k, v, b))
        acc = jnp.zeros((sub, v.shape[1]), F32)
        for s in range(sub):
            d = jnp.where(row_s >= s, bi - bi[s:s + 1], -jnp.inf)
            a = jnp.sum(jnp.exp(d) * qi * ki[s:s + 1], axis=-1, keepdims=True)
            acc = acc + a * vi[s:s + 1]
        parts.append(acc)
    o = o + (parts[0] if len(parts) == 1 else jnp.concatenate(parts, axis=0))
    if c > sub:
        row = lax.broadcasted_iota(I32, (c, dk), 0)
        tt = lax.broadcasted_iota(I32, (c, c), 0)
        ss = lax.broadcasted_iota(I32, (c, c), 1)
        att = jnp.zeros((c, c), F32)
        blk = sub
        while blk < c:
            odd = (row // blk) % 2 == 1
            rq = _block_rows(b, blk, 'prev_end', lambda p: p % 2 == 1)
            rk = _block_rows(b, blk, 'end', lambda p: p % 2 == 0)
            qs = (q * jnp.exp(jnp.where(odd, b - rq, -jnp.inf))).astype(BF16)
            ks = (k * jnp.exp(jnp.where(odd, -jnp.inf, rk - b))).astype(BF16)
            att = att + jnp.where(tt // (2 * blk) == ss // (2 * blk), _dot_nt(qs, ks), 0.0)
            blk *= 2
        o = o + _dot(att.astype(BF16), v.astype(BF16))
    bl = b[c - 1:c]
    kd = k * jnp.exp(bl - b)
    if c % LANES:
        pad = -c % LANES
        v, kd = _pad_rows(v, c + pad), _pad_rows(kd, c + pad)
    st = st * jnp.exp(bl) + _dot(v.T.astype(BF16), kd.astype(BF16))
    return o, st


def _hgrn_kernel(*refs, c, sub, has_s0):
    if has_s0:
        q_ref, k_ref, v_ref, lf_ref, gb_ref, on_ref, s0_ref, o_ref, so_ref, st_ref = refs
    else:
        q_ref, k_ref, v_ref, lf_ref, gb_ref, on_ref, o_ref, so_ref, st_ref = refs
    i = pl.program_id(2)

    @pl.when(i == 0)
    def _():
        st_ref[...] = s0_ref[0, 0].T if has_s0 else jnp.zeros(st_ref.shape, F32)

    def body(ci, carry):
        r0 = pl.multiple_of(ci * c, c)
        sl = pl.ds(r0, c)
        o, st = _hgrn_chunk(q_ref[sl, :], k_ref[sl, :], v_ref[sl, :], lf_ref[sl, :], st_ref[...], sub)
        st_ref[...] = st
        o_ref[sl, :] = (_rms(o, on_ref[...]) * gb_ref[sl, :]).astype(o_ref.dtype)
        return carry

    lax.fori_loop(0, q_ref.shape[0] // c, body, 0)

    @pl.when(i == pl.num_programs(2) - 1)
    def _():
        so_ref[0, 0] = st_ref[...].T


def _hgrn(qh, kk, ib, lf, gb, onorm, s0, s0_base, b, t, nh, tc, c, sub, odtype):
    m, w = qh.shape
    dk = w // nh
    nt = t // tc
    blk = pl.BlockSpec((tc, dk), lambda bb, h, i: (bb * nt + i, h))
    st_spec = pl.BlockSpec((1, 1, dk, dk), lambda bb, h, i: (bb, h, 0, 0))
    ins = [qh, kk, ib, lf, gb, onorm.reshape(1, dk)]
    in_specs = [blk] * 5 + [_const_spec((1, dk))]
    if s0 is not None:
        ins.append(s0)
        in_specs.append(pl.BlockSpec((1, 1, dk, dk), lambda bb, h, i: (s0_base + bb, h, 0, 0)))
    return pl.pallas_call(
        functools.partial(_hgrn_kernel, c=c, sub=sub, has_s0=s0 is not None),
        grid=(b, nh, nt),
        in_specs=in_specs,
        out_specs=[blk, st_spec],
        out_shape=[jax.ShapeDtypeStruct((m, w), odtype), jax.ShapeDtypeStruct((b, nh, dk, dk), F32)],
        scratch_shapes=[pltpu.VMEM((dk, dk), F32)],
        compiler_params=_cp(("arbitrary",) * 3),
        name="hgrn",
    )(*ins)


def _mixer_out(x, parts):
    for a_ref, w_ref in parts:
        x = x + _dot(a_ref[...].astype(BF16), w_ref[...])
    return x


def _mem_attend(q, mk_head, mv_head, nh):
    dh = q.shape[1] // nh
    outs = []
    for h in range(nh):
        s = _dot_nt(q[:, h * dh:(h + 1) * dh], mk_head(h))
        p = jnp.exp(s - jnp.max(s, axis=-1, keepdims=True))
        p = p / jnp.sum(p, axis=-1, keepdims=True)
        outs.append(_dot(p.astype(BF16), mv_head(h)))
    return jnp.concatenate(outs, axis=1).astype(BF16)


def _cross_prompt_kernel(*refs, n_parts, nh, qscale):
    x_ref = refs[0]
    parts = [(refs[1 + 2 * j], refs[2 + 2 * j]) for j in range(n_parts)]
    g_ref, wq_ref, wo_ref, mk_ref, mv_ref, o_ref = refs[1 + 2 * n_parts:]
    x1 = _mixer_out(x_ref[...], parts)
    h = _rms(x1, g_ref[...]).astype(BF16)
    q = (_dot(h, wq_ref[...]) * qscale).astype(BF16)
    dh = q.shape[1] // nh
    o = _mem_attend(q, lambda h: mk_ref[0, :, h * dh:(h + 1) * dh], lambda h: mv_ref[0, :, h * dh:(h + 1) * dh], nh)
    o_ref[...] = x1 + _dot(o, wo_ref[...])


def _cross_prompt(x, parts, g, wq, wo, mk, mv, nh, t, tm):
    m, d = x.shape
    n_mem = mk.shape[1]
    per = t // tm
    row = lambda width: pl.BlockSpec((tm, width), lambda i: (i, 0))
    mem = pl.BlockSpec((1, n_mem, d), lambda i: (i // per, 0, 0))
    ins, in_specs = [x], [row(d)]
    for a, w in parts:
        ins += [a, w]
        in_specs += [row(a.shape[1]), _const_spec(w.shape)]
    ins += [g.reshape(1, d), wq, wo, mk, mv]
    in_specs += [_const_spec((1, d)), _const_spec(wq.shape), _const_spec(wo.shape), mem, mem]
    return pl.pallas_call(
        functools.partial(_cross_prompt_kernel, n_parts=len(parts), nh=nh, qscale=(d // nh) ** -0.5),
        grid=(m // tm,),
        in_specs=in_specs,
        out_specs=row(d),
        out_shape=jax.ShapeDtypeStruct((m, d), F32),
        compiler_params=_cp(("arbitrary",)),
        name="cross_prompt",
    )(*ins)


def _cross_sample_kernel(*refs, n_parts, nh, qscale, td):
    x_ref = refs[0]
    parts = [(refs[1 + 2 * j], refs[2 + 2 * j]) for j in range(n_parts)]
    g_ref, wq_ref, wo_ref, mk_ref, mv_ref, o_ref, x1_ref, q_ref, a_ref, hm_ref = refs[1 + 2 * n_parts:]
    s = pl.program_id(0)
    dh = q_ref.shape[1] // nh

    @pl.when(s == 0)
    def _():
        x1 = _mixer_out(x_ref[...], parts)
        x1_ref[...] = x1
        q_ref[...] = _dot(_rms(x1, g_ref[...]).astype(BF16), wq_ref[...]) * qscale
        hm_ref[...] = _head_mask_bias(nh * td, hm_ref.shape[1], nh, td)

    sl = pl.ds(pl.multiple_of(s * td, td), td)
    qs = q_ref[sl, :].astype(BF16)
    qall = jnp.concatenate([qs[:, h * dh:(h + 1) * dh] for h in range(nh)], axis=0)
    sc = _dot_nt(qall, mk_ref[...].astype(BF16)) + hm_ref[...]
    p = jnp.exp(sc - jnp.max(sc, axis=-1, keepdims=True))
    p = p / jnp.sum(p, axis=-1, keepdims=True)
    o = _dot(p.astype(BF16), mv_ref[...].astype(BF16))
    a_ref[sl, :] = jnp.concatenate([o[h * td:(h + 1) * td] for h in range(nh)], axis=1)

    @pl.when(s == pl.num_programs(0) - 1)
    def _():
        o_ref[...] = x1_ref[...] + _dot(a_ref[...].astype(BF16), wo_ref[...])


def _cross_sample(x, parts, g, wq, wo, mk, mv, base, n_mem, nh, bd, td):
    m, d = x.shape
    assert nh & (nh - 1) == 0 and td & (td - 1) == 0
    mem = pl.BlockSpec((n_mem * nh, d // nh), lambda s: (base + s, 0))
    ins, in_specs = [x], [_const_spec((m, d))]
    for a, w in parts:
        ins += [a, w]
        in_specs += [_const_spec(a.shape), _const_spec(w.shape)]
    ins += [g.reshape(1, d), wq, wo, mk, mv]
    in_specs += [_const_spec((1, d)), _const_spec(wq.shape), _const_spec(wo.shape), mem, mem]
    return pl.pallas_call(
        functools.partial(_cross_sample_kernel, n_parts=len(parts), nh=nh, qscale=(d // nh) ** -0.5, td=td),
        grid=(bd,),
        in_specs=in_specs,
        out_specs=_const_spec((m, d)),
        out_shape=jax.ShapeDtypeStruct((m, d), F32),
        scratch_shapes=[pltpu.VMEM((m, d), F32)] * 3 + [pltpu.VMEM((nh * td, n_mem * nh), F32)],
        compiler_params=_cp(("arbitrary",)),
        name="cross_sample",
    )(*ins)


def _ffn_kernel(*refs, carry_mode, seq_rows, blocks_per_seq, nchunk, cw, final_norm):
    if carry_mode:
        (x_ref, g_ref, wi_ref, cwt_ref, cb_ref, wo_ref, gf_ref, o_ref, st_ref, prev_ref) = refs
    else:
        (x_ref, g_ref, wi_ref, cwt_ref, cb_ref, wo_ref, gf_ref, f1_ref, f2_ref, o_ref, gate_ref) = refs
    x = x_ref[...]
    tm = x.shape[0]
    dff = cwt_ref.shape[1]
    h = _rms(x, g_ref[...]).astype(BF16)
    row = lax.broadcasted_iota(I32, (tm, cw), 0)
    if carry_mode:
        @pl.when(pl.program_id(0) % blocks_per_seq == 0)
        def _():
            prev_ref[...] = jnp.zeros(prev_ref.shape, F32)
    else:
        rseq = row % seq_rows
    acc = jnp.zeros(x.shape, F32)
    for ci in range(nchunk):
        c0 = ci * cw
        cs = slice(c0, c0 + cw)
        gt = _dot(h, wi_ref[:, cs])
        u = _dot(h, wi_ref[:, dff + c0:dff + c0 + cw])
        if carry_mode:
            p0, p1 = prev_ref[0:1, cs], prev_ref[1:2, cs]
            g1 = jnp.where(row == 0, p1, pltpu.roll(gt, 1, 0))
            g2 = jnp.where(row == 0, p0, jnp.where(row == 1, p1, pltpu.roll(gt, 2, 0)))
            last = gt[tm - 2:tm]
            prev_ref[:, cs] = last
            st_ref[0, :, cs] = last
        else:
            g1 = jnp.where(rseq >= 1, pltpu.roll(gt, 1, 0), f1_ref[:, cs])
            g2 = jnp.where(rseq >= 2, pltpu.roll(gt, 2, 0), f2_ref[:, cs])
            gate_ref[:, cs] = gt
        gc = cb_ref[:, cs] + g2 * cwt_ref[0:1, cs] + g1 * cwt_ref[1:2, cs] + gt * cwt_ref[2:3, cs]
        acc = acc + _dot((_silu(gc) * u).astype(BF16), wo_ref[cs, :])
    y = x + acc
    if final_norm:
        y = _rms(y, gf_ref[...])
    o_ref[...] = y


def _ffn(x, g, wi, cwt, cb, wo, gfinal, final_norm, seq_rows, tm, fills=None):
    m, d = x.shape
    dff = wo.shape[0]
    cw = 256
    carry_mode = fills is None
    nseq = m // seq_rows
    row = lambda width: pl.BlockSpec((tm, width), lambda i: (i, 0))
    ins = [x, g.reshape(1, d), wi, cwt, cb.reshape(1, dff), wo, gfinal.reshape(1, d)]
    in_specs = [row(d), _const_spec((1, d)), _const_spec(wi.shape), _const_spec(cwt.shape),
                _const_spec((1, dff)), _const_spec(wo.shape), _const_spec((1, d))]
    if carry_mode:
        per = seq_rows // tm
        out_specs = [row(d), pl.BlockSpec((1, 2, dff), lambda i: (i // per, 0, 0))]
        out_shape = [jax.ShapeDtypeStruct((m, d), F32), jax.ShapeDtypeStruct((nseq, 2, dff), F32)]
        scratch = [pltpu.VMEM((2, dff), F32)]
    else:
        per = 1
        ins += list(fills)
        in_specs += [row(dff), row(dff)]
        out_specs = [row(d), row(dff)]
        out_shape = [jax.ShapeDtypeStruct((m, d), F32), jax.ShapeDtypeStruct((m, dff), F32)]
        scratch = []
    return pl.pallas_call(
        functools.partial(_ffn_kernel, carry_mode=carry_mode, seq_rows=seq_rows, blocks_per_seq=per,
                          nchunk=dff // cw, cw=cw, final_norm=final_norm),
        grid=(m // tm,),
        in_specs=in_specs,
        out_specs=out_specs,
        out_shape=out_shape,
        scratch_shapes=scratch,
        compiler_params=_cp(("arbitrary",)),
        name="ffn",
    )(*ins)


def _sort_key(x):
    bits = lax.bitcast_convert_type(x, I32)
    key = jnp.where(bits < 0, bits ^ 0x7FFFFFFF, bits)
    return jnp.where(x == 0.0, 0, key)


def _fold_lanes(c):
    parts = [c[:, j * LANES:(j + 1) * LANES] for j in range(c.shape[1] // LANES)]
    while len(parts) > 1:
        parts = [parts[j] + parts[j + 1] for j in range(0, len(parts) - 1, 2)] + (parts[-1:] if len(parts) % 2 else [])
    return parts[0]


def _select_topk(keys_ref, n_tiles, topk, idx_bits):
    _, rows, tw = keys_ref.shape
    lane = lax.broadcasted_iota(I32, (rows, tw), 1)

    def count(ind):
        def body(j, acc):
            return acc + _fold_lanes(ind(keys_ref[j], j))
        acc = lax.fori_loop(0, n_tiles, body, jnp.zeros((rows, LANES), F32))
        return jnp.sum(acc, axis=-1, keepdims=True)

    def thr_step(it, thr):
        trial = thr ^ jnp.left_shift(jnp.int32(1), 31 - it)
        cnt = count(lambda kt, j: jnp.where(kt >= trial, 1.0, 0.0))
        return jnp.where(cnt >= topk, trial, thr)

    thr = lax.fori_loop(0, 32, thr_step, jnp.full((rows, 1), INT_MIN, I32))
    thr = jnp.maximum(thr, INT_MIN + 1)
    need = topk - count(lambda kt, j: jnp.where(kt > thr, 1.0, 0.0))
    ties = count(lambda kt, j: jnp.where(kt == thr, 1.0, 0.0))
    excess = jnp.max(ties - need) > 0.0

    def cut_step(it, cut):
        trial = cut | jnp.left_shift(jnp.int32(1), idx_bits - 1 - it)
        cnt = count(lambda kt, j: jnp.where(kt == thr, jnp.where(lane + j * tw < trial, 1.0, 0.0), 0.0))
        return jnp.where(cnt <= need, trial, cut)

    cut0 = jnp.full((rows, 1), jnp.where(excess, 0, 1 << idx_bits), I32)
    cut = lax.fori_loop(0, jnp.where(excess, idx_bits, 0), cut_step, cut0)
    return thr, cut


def _selection_bias(kt, j, tw, thr, cut):
    lane = lax.broadcasted_iota(I32, kt.shape, 1)
    return jnp.where(kt > thr, 0.0, jnp.where(kt == thr, jnp.where(lane + j * tw < cut, 0.0, NEG), NEG))


def _index_scores(iq, iw, ik2_tile, nhi, di):
    lane = lax.broadcasted_iota(I32, (iq.shape[0], LANES), 1)
    acc = None
    for pr in range(nhi * di // LANES):
        pair = iq[:, pr * LANES:(pr + 1) * LANES]
        zero = jnp.zeros_like(pair)
        for half_i, qm in enumerate((jnp.where(lane < di, pair, zero), jnp.where(lane >= di, pair, zero))):
            hd = 2 * pr + half_i
            term = jnp.maximum(_dot_nt(qm, ik2_tile), 0.0) * iw[:, hd:hd + 1]
            acc = term if acc is None else acc + term
    return acc


def _dsa_index_kernel(iq_ref, tail_ref, ik2_ref, bias_ref, keys_ref, *, tq, tw, topk, nhi, di, idx_bits):
    i = pl.program_id(1)
    nt_all = keys_ref.shape[0]
    n_tiles = (i * tq + tq + tw - 1) // tw
    iq = iq_ref[...]
    iw = tail_ref[:, di:di + nhi]
    qpos = i * tq + lax.broadcasted_iota(I32, (tq, tw), 0)
    lane = lax.broadcasted_iota(I32, (tq, tw), 1)

    def score_tile(j, c):
        r0 = pl.multiple_of(j * tw, tw)
        sc = _index_scores(iq, iw, ik2_ref[pl.ds(r0, tw), :], nhi, di)
        keys_ref[j] = jnp.where(lane + j * tw <= qpos, _sort_key(sc), INT_MIN)
        return c

    lax.fori_loop(0, n_tiles, score_tile, 0)
    thr, cut = _select_topk(keys_ref, n_tiles, topk, idx_bits)

    def write_tile(j, c):
        bias_ref[0, j] = _selection_bias(keys_ref[j], j, tw, thr, cut).astype(bias_ref.dtype)
        return c

    lax.fori_loop(0, n_tiles, write_tile, 0)

    def fill_tile(j, c):
        bias_ref[0, j] = jnp.full((tq, tw), NEG, bias_ref.dtype)
        return c

    lax.fori_loop(n_tiles, nt_all, fill_tile, 0)


def _dsa_index_prompt(iq, tail, ik2, b, t, topk, nhi, di, tq, tw):
    m = iq.shape[0]
    nq = t // tq
    nt = t // tw
    return pl.pallas_call(
        functools.partial(_dsa_index_kernel, tq=tq, tw=tw, topk=topk, nhi=nhi, di=di,
                          idx_bits=max(1, t.bit_length())),
        grid=(b, nq),
        in_specs=[pl.BlockSpec((tq, iq.shape[1]), lambda bb, i: (bb * nq + i, 0)),
                  pl.BlockSpec((tq, LANES), lambda bb, i: (bb * nq + i, 0)),
                  pl.BlockSpec((t, LANES), lambda bb, i: (bb, 0))],
        out_specs=pl.BlockSpec((1, nt, tq, tw), lambda bb, i: (bb * nq + i, 0, 0, 0)),
        out_shape=jax.ShapeDtypeStruct((m // tq, nt, tq, tw), BF16),
        scratch_shapes=[pltpu.VMEM((nt, tq, tw), I32)],
        compiler_params=_cp(("arbitrary", "arbitrary")),
        name="dsa_index",
    )(iq, tail, ik2)


def _gqa_update(g, s, vlist, m_ref, l_ref, a_ref):
    m_old = m_ref[g]
    m_new = jnp.maximum(m_old, jnp.max(s, axis=-1, keepdims=True))
    alpha = jnp.exp(m_old - m_new)
    p = jnp.exp(s - m_new[:, :1])
    l_ref[g] = alpha * l_ref[g] + jnp.sum(p, axis=-1, keepdims=True)
    pb = p.astype(BF16)
    kw = s.shape[1] // len(vlist)
    pv = _dot(pb[:, :kw], vlist[0])
    for r in range(1, len(vlist)):
        pv = pv + _dot(pb[:, r * kw:(r + 1) * kw], vlist[r])
    a_ref[g] = alpha * a_ref[g] + pv
    m_ref[g] = m_new


def _stack_heads(q, g, per, dh):
    return jnp.concatenate([q[:, (g * per + hl) * dh:(g * per + hl + 1) * dh] for hl in range(per)], axis=0)


def _unstack_heads(a_ref, l_ref, nkv, per, rows):
    outs = []
    for g in range(nkv):
        o = a_ref[g] / l_ref[g][:, :1]
        outs += [o[hl * rows:(hl + 1) * rows] for hl in range(per)]
    return jnp.concatenate(outs, axis=1)


def _dsa_attn_kernel(q_ref, k_ref, v_ref, bias_ref, o_ref, m_ref, l_ref, a_ref, *, tq, tw, nkv, per, dh):
    i = pl.program_id(1)
    n_tiles = (i * tq + tq + tw - 1) // tw
    m_ref[...] = jnp.full(m_ref.shape, NEG, F32)
    l_ref[...] = jnp.zeros(l_ref.shape, F32)
    a_ref[...] = jnp.zeros(a_ref.shape, F32)
    q = q_ref[...]
    qg = [_stack_heads(q, g, per, dh) for g in range(nkv)]

    def body(j, c):
        r0 = pl.multiple_of(j * tw, tw)
        bias = bias_ref[0, j].astype(F32)
        bias = jnp.concatenate([bias] * per, axis=0)
        for g in range(nkv):
            kb = k_ref[pl.ds(r0, tw), g * dh:(g + 1) * dh]
            vb = v_ref[pl.ds(r0, tw), g * dh:(g + 1) * dh]
            _gqa_update(g, _dot_nt(qg[g], kb) + bias, [vb], m_ref, l_ref, a_ref)
        return c

    lax.fori_loop(0, n_tiles, body, 0)
    o_ref[...] = _unstack_heads(a_ref, l_ref, nkv, per, tq).astype(o_ref.dtype)


def _dsa_attn_prompt(q, kb, vb, bias, b, t, nkv, tq, tw):
    m, w = q.shape
    dh = kb.shape[1] // nkv
    per = w // dh // nkv
    nq = t // tq
    nt = t // tw
    return pl.pallas_call(
        functools.partial(_dsa_attn_kernel, tq=tq, tw=tw, nkv=nkv, per=per, dh=dh),
        grid=(b, nq),
        in_specs=[pl.BlockSpec((tq, w), lambda bb, i: (bb * nq + i, 0)),
                  pl.BlockSpec((t, nkv * dh), lambda bb, i: (bb, 0)),
                  pl.BlockSpec((t, nkv * dh), lambda bb, i: (bb, 0)),
                  pl.BlockSpec((1, nt, tq, tw), lambda bb, i: (bb * nq + i, 0, 0, 0))],
        out_specs=pl.BlockSpec((tq, w), lambda bb, i: (bb * nq + i, 0)),
        out_shape=jax.ShapeDtypeStruct((m, w), BF16),
        scratch_shapes=[pltpu.VMEM((nkv, per * tq, LANES), F32), pltpu.VMEM((nkv, per * tq, LANES), F32),
                        pltpu.VMEM((nkv, per * tq, dh), F32)],
        compiler_params=_cp(("arbitrary", "arbitrary")),
        name="dsa_attn",
    )(q, kb, vb, bias)


def _dsa_dec_scores_kernel(pt_ref, iq_ref, iw_ref, ikn_ref, *rest, G, td, nhi, page, wnew):
    ikp = rest[:G]
    past_ref, new_ref = rest[G:]
    iq = iq_ref[0].astype(BF16)
    iw = iw_ref[0]

    def scores(ik):
        s = jnp.maximum(_dot_nt(iq, ik), 0.0) * iw
        out = s[0:td]
        for h in range(1, nhi):
            out = out + s[h * td:(h + 1) * td]
        return out

    past_ref[0] = jnp.concatenate([scores(ikp[r][0].astype(BF16)) for r in range(G)], axis=1)

    @pl.when(pl.program_id(1) == pl.num_programs(1) - 1)
    def _():
        sn = scores(_pad_rows(ikn_ref[...], wnew).astype(BF16))
        row = lax.broadcasted_iota(I32, sn.shape, 0)
        col = lax.broadcasted_iota(I32, sn.shape, 1)
        new_ref[0] = jnp.where(col <= row, sn, -jnp.inf)


def _dsa_dec_scores(pt, iq_hm, iw_hm, ik_new, pool_ik, base, bd, td, nhi, G, wnew):
    di = pool_ik.shape[2]
    page = pool_ik.shape[1]
    npg = pt.shape[1] // G
    seq3 = lambda shape: pl.BlockSpec((1,) + shape, lambda s, p, pt_: (s, 0, 0))
    grid_spec = pltpu.PrefetchScalarGridSpec(
        num_scalar_prefetch=1,
        grid=(bd, npg),
        in_specs=[seq3((nhi * td, di)), seq3((nhi * td, 1)), pl.BlockSpec((td, di), lambda s, p, pt_: (s, 0))]
                 + [pl.BlockSpec((1, page, di), functools.partial(lambda s, p, pt_, r: (base + pt_[s, p * G + r], 0, 0), r=r))
                    for r in range(G)],
        out_specs=[pl.BlockSpec((1, td, G * page), lambda s, p, pt_: (s, 0, p)), seq3((td, wnew))],
    )
    return pl.pallas_call(
        functools.partial(_dsa_dec_scores_kernel, G=G, td=td, nhi=nhi, page=page, wnew=wnew),
        grid_spec=grid_spec,
        out_shape=[jax.ShapeDtypeStruct((bd, td, pt.shape[1] * page), F32), jax.ShapeDtypeStruct((bd, td, wnew), F32)],
        compiler_params=_cp(("arbitrary", "arbitrary")),
        name="dsa_dec_scores",
    )(pt, iq_hm, iw_hm, ik_new, *([pool_ik] * G))


def _dsa_dec_select_kernel(sc_ref, bias_ref, keys_ref, *, topk, idx_bits):
    nt, rows, tw = keys_ref.shape
    for j in range(nt):
        sc = sc_ref[j]
        keys_ref[j] = jnp.where(sc == -jnp.inf, INT_MIN, _sort_key(sc))
    thr, cut = _select_topk(keys_ref, nt, topk, idx_bits)
    for j in range(nt):
        bias_ref[j] = _selection_bias(keys_ref[j], j, tw, thr, cut)


def _dsa_dec_select(sc_tiles, topk, rows):
    nt, m, tw = sc_tiles.shape
    spec = pl.BlockSpec((nt, rows, tw), lambda i: (0, i, 0))
    return pl.pallas_call(
        functools.partial(_dsa_dec_select_kernel, topk=topk, idx_bits=max(1, (nt * tw).bit_length())),
        grid=(m // rows,),
        in_specs=[spec],
        out_specs=spec,
        out_shape=jax.ShapeDtypeStruct((nt, m, tw), F32),
        scratch_shapes=[pltpu.VMEM((nt, rows, tw), I32)],
        compiler_params=_cp(("arbitrary",)),
        name="dsa_dec_select",
    )(sc_tiles)


def _dsa_dec_attn_kernel(pt_ref, q_ref, kn_ref, vn_ref, bias_ref, bnew_ref, *rest, G, td, nh, nkv):
    kp, vp = rest[:G], rest[G:2 * G]
    o_ref, m_ref, l_ref, a_ref, hm_ref = rest[2 * G:]
    p = pl.program_id(1)
    dh = q_ref.shape[1] // nh
    rows = nh * td

    @pl.when(p == 0)
    def _():
        _init_softmax_state(m_ref, l_ref, a_ref)
        hm_ref[...] = _head_mask_bias(rows, hm_ref.shape[1], nkv, rows // nkv)

    q = q_ref[...].astype(BF16)
    qall = jnp.concatenate([q[:, h * dh:(h + 1) * dh] for h in range(nh)], axis=0)
    bias = jnp.concatenate([bias_ref[j] for j in range(bias_ref.shape[0])], axis=1)
    bias = jnp.concatenate([bias] * nh, axis=0) + hm_ref[...]
    s = jnp.concatenate([_dot_nt(qall, kp[r][...].astype(BF16)) for r in range(G)], axis=1) + bias
    _online_update(s, [vp[r][...].astype(BF16) for r in range(G)], m_ref, l_ref, a_ref)

    @pl.when(p == pl.num_programs(1) - 1)
    def _():
        kn = _pad_rows(kn_ref[...], LANES).astype(BF16)
        vn = _pad_rows(vn_ref[...], LANES).astype(BF16)
        bias_n = jnp.concatenate([bnew_ref[0, :, 0:LANES]] * nh, axis=0) + hm_ref[:, 0:LANES]
        _online_update(_dot_nt(qall, kn) + bias_n, [vn], m_ref, l_ref, a_ref)
        o = a_ref[...] / l_ref[...][:, :1]
        o_ref[...] = jnp.concatenate([o[h * td:(h + 1) * td] for h in range(nh)], axis=1)


def _dsa_dec_attn(pt, q, kn, vn, bias2, pool_k, pool_v, base, page, bd, td, nkv, G):
    m, w = q.shape
    dh = pool_k.shape[1]
    nh = w // dh
    assert dh == LANES and nkv & (nkv - 1) == 0 and td * nkv <= LANES
    npg = pt.shape[1] // G
    nt, _, tw2 = bias2.shape
    tiles_per_step = G * page * nkv // tw2
    rows = nh * td
    seq = pl.BlockSpec((td, w), lambda s, p, pt_: (s, 0))
    new = pl.BlockSpec((td * nkv, dh), lambda s, p, pt_: (s, 0))

    def page_spec(r):
        return pl.BlockSpec((page * nkv, dh), lambda s, p, pt_: (base + pt_[s, p * G + r], 0))

    grid_spec = pltpu.PrefetchScalarGridSpec(
        num_scalar_prefetch=1,
        grid=(bd, npg),
        in_specs=[seq, new, new,
                  pl.BlockSpec((tiles_per_step, td, tw2), lambda s, p, pt_: (p, s, 0)),
                  pl.BlockSpec((1, td, tw2), lambda s, p, pt_: (nt - 1, s, 0))]
                 + [page_spec(r) for r in range(G)] * 2,
        out_specs=seq,
        scratch_shapes=[pltpu.VMEM((rows, LANES), F32), pltpu.VMEM((rows, LANES), F32), pltpu.VMEM((rows, dh), F32),
                        pltpu.VMEM((rows, G * page * nkv), F32)],
    )
    return pl.pallas_call(
        functools.partial(_dsa_dec_attn_kernel, G=G, td=td, nh=nh, nkv=nkv),
        grid_spec=grid_spec,
        out_shape=jax.ShapeDtypeStruct((m, w), F32),
        compiler_params=_cp(("arbitrary", "arbitrary")),
        name="dsa_dec_attn",
    )(pt, q, kn, vn, bias2, bias2, *([pool_k] * G), *([pool_v] * G))


def kernel(x_prompt, x_sample, cache_diff_k, cache_diff_v, state_hgrn, cache_dsa_k, cache_dsa_v, cache_dsa_ik,
           cache_mem_k, cache_mem_v, state_ffn_conv, page_table, mem_prompt, norm_mix, w_in_ab, diff_lq1, diff_lk1,
           diff_lq2, diff_lk2, diff_subln, hgrn_lb_logits, hgrn_onorm, w_out_ab, w_in_c, idx_k_norm, w_out_c,
           norm_x, norm_mem, w_xq, w_xk, w_xv, w_xo, norm_ffn, w_ffn_in, conv_ffn_w, conv_ffn_b, w_ffn_out,
           norm_final):
    b, t, d = x_prompt.shape
    bd, td, _ = x_sample.shape
    depth = norm_mix.shape[0]
    n_ab = w_in_ab.shape[0]
    ha, dva = cache_diff_v.shape[3], cache_diff_v.shape[4]
    dqa = cache_diff_k.shape[4] // 2
    hb, dkb = state_hgrn.shape[2], state_hgrn.shape[3]
    kvc, dhc = cache_dsa_k.shape[3], cache_dsa_k.shape[4]
    hc = d // dhc
    di = cache_dsa_ik.shape[3]
    hi = w_in_c.shape[2] - (hc + 2 * kvc) * dhc - di
    hi = hi // (di + 1)
    hx = cache_mem_k.shape[3]
    n_mem = mem_prompt.shape[1]
    dff = w_ffn_out.shape[1]
    n_pool, page = cache_diff_k.shape[1], cache_diff_k.shape[2]
    past = page_table.shape[1] * page
    mp, ms = b * t, bd * td
    G = 8

    pos_p = jnp.tile(jnp.arange(t), b)
    pos_s = past + jnp.tile(jnp.arange(td), bd)
    tab64_p, half64 = _rope_tables(pos_p, dqa)
    tab64_s, _ = _rope_tables(pos_s, dqa)
    tab128_p, half128 = _rope_tables(pos_p, dhc)
    tab128_s, _ = _rope_tables(pos_s, dhc)

    hgrn_lb = jnp.cumsum(jax.nn.softmax(hgrn_lb_logits.astype(F32), axis=0), axis=0)[:n_ab]
    xp = x_prompt.reshape(mp, d)
    xs = x_sample.reshape(ms, d)
    gfin = norm_final

    mkf, mvf, mkb, mvb = _memproj(mem_prompt.reshape(b * n_mem, d), norm_mem, w_xk.astype(BF16), w_xv.astype(BF16), 256)

    new_p = {k: [] for k in ('diff_k', 'diff_v', 'hgrn', 'dsa_k', 'dsa_v', 'dsa_ik', 'conv')}
    new_s = {k: [] for k in new_p}

    for l in range(depth):
        j = l // 2
        wq_b, wo_b = w_xq[l].astype(BF16), w_xo[l].astype(BF16)
        if l % 2 == 0:
            w_in = w_in_ab[j].astype(BF16)
            w_out = w_out_ab[j].astype(BF16)
            wa, wb = w_out[:ha * dva], w_out[ha * dva:]
            lam_init = 0.8 - 0.6 * math.exp(-0.3 * l)
            lam = (jnp.exp(jnp.sum(diff_lq1[j].astype(F32) * diff_lk1[j].astype(F32)))
                   - jnp.exp(jnp.sum(diff_lq2[j].astype(F32) * diff_lk2[j].astype(F32))) + lam_init)
            q, kf, vf, kb, vb, qh, kk, lf, ib, gb = _inproj_ab(
                xp, norm_mix[l], w_in, hgrn_lb[j], tab64_p, half64, dqa, dkb, BF16, 256)
            o_a = _diffattn_prompt(lam, q, kb, vb, diff_subln[j], b, t, ha, dqa, 1.0 - lam_init, 256)
            o_b, s_new = _hgrn(qh, kk, ib, lf, gb, hgrn_onorm[j], None, 0, b, t, hb, 512, 128, 8, BF16)
            new_p['diff_k'].append(kf.reshape(1, b, t, ha, 2 * dqa))
            new_p['diff_v'].append(vf.reshape(1, b, t, ha, dva))
            new_p['hgrn'].append(s_new[None])
            parts_p = [(o_a, wa), (o_b, wb)]
            q, kf, vf, kb, vb, qh, kk, lf, ib, gb = _inproj_ab(
                xs, norm_mix[l], w_in, hgrn_lb[j], tab64_s, half64, dqa, dkb, F32, ms)
            pool_k = cache_diff_k.reshape(-1, 2 * dqa)
            pool_v = cache_diff_v.reshape(-1, dva)
            o_a = _diff_decode(page_table, lam, q, kf.reshape(-1, 2 * dqa), vf.reshape(-1, dva), diff_subln[j],
                               pool_k, pool_v, j * n_pool, page, bd, td, ha, dqa, 1.0 - lam_init, G)
            c_s = math.gcd(td, 64)
            o_b, s_new = _hgrn(qh, kk, ib, lf, gb, hgrn_onorm[j], state_hgrn.reshape(-1, hb, dkb, dkb), j * bd,
                               bd, td, hb, td, c_s, min(8, c_s), F32)
            new_s['diff_k'].append(kf.reshape(1, bd, td, ha, 2 * dqa))
            new_s['diff_v'].append(vf.reshape(1, bd, td, ha, dva))
            new_s['hgrn'].append(s_new[None])
            parts_s = [(o_a, wa), (o_b, wb)]
        else:
            w_in = w_in_c[j].astype(BF16)
            cw = w_in.shape[1]
            main = (hc + 2 * kvc) * dhc + hi * di
            w_pad = jnp.concatenate([w_in, jnp.zeros((d, main + LANES - cw), BF16)], axis=1)
            ikg_pad = jnp.concatenate([idx_k_norm[j].astype(F32), jnp.zeros((LANES - di,), F32)]).reshape(1, LANES)
            dims = (hc * dhc, kvc * dhc, hi * di, di, hi, dhc)
            w_out = w_out_c[j].astype(BF16)
            q, kf, vf, kb, vb, iq, tail, ik2 = _inproj_c(
                xp, norm_mix[l], w_pad, ikg_pad, tab128_p, tab64_p, half128, half64, dims, BF16, 256)
            topk = min(DSA_TOPK_MAX, t // 4)
            bias = _dsa_index_prompt(iq, tail, ik2, b, t, topk, hi, di, 128, 512)
            o_c = _dsa_attn_prompt(q, kb, vb, bias, b, t, kvc, 128, 512)
            new_p['dsa_k'].append(kf.reshape(1, b, t, kvc, dhc))
            new_p['dsa_v'].append(vf.reshape(1, b, t, kvc, dhc))
            new_p['dsa_ik'].append(tail[:, :di].reshape(1, b, t, di))
            parts_p = [(o_c, w_out)]
            q, kf, vf, kb, vb, iq, tail, ik2 = _inproj_c(
                xs, norm_mix[l], w_pad, ikg_pad, tab128_s, tab64_s, half128, half64, dims, F32, ms)
            tw = 512
            iq_hm = iq.reshape(bd, td, hi, di).transpose(0, 2, 1, 3).reshape(bd, hi * td, di)
            iw_hm = tail[:, di:di + hi].reshape(bd, td, hi).transpose(0, 2, 1).reshape(bd, hi * td, 1)
            ik_new = tail[:, :di]
            sc_past, sc_new = _dsa_dec_scores(page_table, iq_hm, iw_hm, ik_new, cache_dsa_ik.reshape(-1, page, di),
                                              j * n_pool, bd, td, hi, G, tw)
            sc = jnp.concatenate([sc_past, sc_new], axis=2).reshape(ms, -1)
            nt = sc.shape[1] // tw
            sc_tiles = sc.reshape(ms, nt, tw).transpose(1, 0, 2)
            topk = min(DSA_TOPK_MAX, (past + td) // 4)
            bias_tiles = _dsa_dec_select(sc_tiles, topk, 128 if ms % 128 == 0 else ms)
            bias2 = jnp.repeat(bias_tiles, kvc, axis=2)
            o_c = _dsa_dec_attn(page_table, q, kf.reshape(-1, dhc), vf.reshape(-1, dhc), bias2,
                                cache_dsa_k.reshape(-1, dhc), cache_dsa_v.reshape(-1, dhc), j * n_pool, page, bd, td,
                                kvc, G)
            new_s['dsa_k'].append(kf.reshape(1, bd, td, kvc, dhc))
            new_s['dsa_v'].append(vf.reshape(1, bd, td, kvc, dhc))
            new_s['dsa_ik'].append(ik_new.reshape(1, bd, td, di))
            parts_s = [(o_c, w_out)]

        last = l == depth - 1
        wi_b, wo2_b = w_ffn_in[l].astype(BF16), w_ffn_out[l].astype(BF16)
        xp = _cross_prompt(xp, parts_p, norm_x[l], wq_b, wo_b, mkb[l].reshape(b, n_mem, d), mvb[l].reshape(b, n_mem, d),
                           hx, t, 256)
        xp, conv_p = _ffn(xp, norm_ffn[l], wi_b, conv_ffn_w[l], conv_ffn_b[l], wo2_b, gfin, last, t, 256)
        new_p['conv'].append(conv_p[None])
        xs = _cross_sample(xs, parts_s, norm_x[l], wq_b, wo_b, cache_mem_k.reshape(-1, d // hx),
                           cache_mem_v.reshape(-1, d // hx), l * bd, n_mem, hx, bd, td)
        st = state_ffn_conv[l]
        zeros = jnp.zeros((bd, td - 1, dff), F32)
        fill1 = jnp.concatenate([st[:, 1:2], zeros], axis=1).reshape(ms, dff)
        fill2 = jnp.concatenate([st, zeros[:, 1:]], axis=1).reshape(ms, dff)
        xs, gate_s = _ffn(xs, norm_ffn[l], wi_b, conv_ffn_w[l], conv_ffn_b[l], wo2_b, gfin, last, td, ms,
                          fills=(fill1, fill2))
        new_s['conv'].append(gate_s.reshape(bd, td, dff)[None, :, td - 2:])

    join = lambda v: v[0] if len(v) == 1 else jnp.concatenate(v, axis=0)
    sp = {k: join(v) for k, v in new_p.items()}
    ss = {k: join(v) for k, v in new_s.items()}
    mk_p = mkf.reshape(depth, b, n_mem, hx, d // hx)
    mv_p = mvf.reshape(depth, b, n_mem, hx, d // hx)
    return (xp.reshape(b, t, d), xs.reshape(bd, td, d),
            sp['diff_k'], sp['diff_v'], sp['hgrn'], sp['dsa_k'], sp['dsa_v'], sp['dsa_ik'],
            mk_p, mv_p, sp['conv'],
            ss['diff_k'], ss['diff_v'], ss['hgrn'], ss['dsa_k'], ss['dsa_v'], ss['dsa_ik'], ss['conv'])
```

```python
import functools
import math

import jax
import jax.numpy as jnp
import numpy as np
from jax import lax
from jax.experimental import pallas as pl
from jax.experimental.pallas import tpu as pltpu

F32 = jnp.float32
BF16 = jnp.bfloat16
I32 = jnp.int32

EPS = 1e-6
ROPE_THETA = 500000.0
ROT_DIV = 4
DSA_TOPK_MAX = 256
NEG = -1e30
INT_MIN = -2147483648
LANES = 128
VMEM_LIMIT = 56 * 1024 * 1024

NT = (((1,), (1,)), ((), ()))


def _cp(sem):
    return pltpu.CompilerParams(dimension_semantics=sem, vmem_limit_bytes=VMEM_LIMIT)


def _dot(a, b):
    return jnp.dot(a, b, preferred_element_type=F32)


def _dot_nt(a, b):
    return lax.dot_general(a, b, NT, preferred_element_type=F32)


def _rms(x, g):
    ms = jnp.mean(x * x, axis=-1, keepdims=True)
    return x * lax.rsqrt(ms + EPS) * g


def _silu(x):
    return x * jax.nn.sigmoid(x)


def _tile_lanes(t, n):
    return t if n == 1 else jnp.concatenate([t] * n, axis=1)


def _rope(z, c, s1, s2, half):
    w = z.shape[1]
    n = w // LANES
    return (z * _tile_lanes(c, n) + pltpu.roll(z, half, 1) * _tile_lanes(s1, n)
            + pltpu.roll(z, w - half, 1) * _tile_lanes(s2, n))


def _rope_tables(pos, head_dim):
    half = head_dim // ROT_DIV // 2
    inv = jnp.exp(jnp.arange(half, dtype=F32) * (-math.log(ROPE_THETA) / half))
    ang = pos.astype(F32)[:, None] * inv[None, :]
    cos, sin = jnp.cos(ang), jnp.sin(ang)
    m = pos.shape[0]
    rest = head_dim - 2 * half
    c = jnp.concatenate([cos, cos, jnp.ones((m, rest), F32)], axis=1)
    s1 = jnp.concatenate([jnp.zeros((m, half), F32), sin, jnp.zeros((m, rest), F32)], axis=1)
    s2 = jnp.concatenate([-sin, jnp.zeros((m, half + rest), F32)], axis=1)
    rep = LANES // head_dim
    return tuple(jnp.tile(t, (1, rep)) for t in (c, s1, s2)), half


def _store_heads(ref, x):
    dim = ref.shape[2]
    for h in range(ref.shape[1]):
        ref[:, h, :] = x[:, h * dim:(h + 1) * dim]


def _const_spec(shape):
    nd = len(shape)
    return pl.BlockSpec(shape, lambda *_: (0,) * nd)


def _inproj_ab_kernel(x_ref, g_ref, w_ref, lb_ref, c_ref, s1_ref, s2_ref,
                      q_ref, kf_ref, vf_ref, kb_ref, vb_ref, qh_ref, kk_ref, lf_ref, ib_ref, gb_ref,
                      *, half, qscale, hscale):
    h = _rms(x_ref[...], g_ref[...]).astype(BF16)
    sw = q_ref.shape[1]

    def seg(i):
        return _dot(h, w_ref[:, i * sw:(i + 1) * sw])

    c, s1, s2 = c_ref[...], s1_ref[...], s2_ref[...]
    q_ref[...] = (_rope(seg(0), c, s1, s2, half) * qscale).astype(q_ref.dtype)
    ka = _rope(seg(1), c, s1, s2, half)
    _store_heads(kf_ref, ka)
    kb_ref[...] = ka.astype(BF16)
    va = seg(2)
    _store_heads(vf_ref, va)
    vb_ref[...] = va.astype(BF16)
    qh_ref[...] = _silu(seg(3)) * hscale
    lb = lb_ref[...]
    fg = lb + (1.0 - lb) * jax.nn.sigmoid(seg(4))
    kk_ref[...] = 1.0 - fg
    lf_ref[...] = jnp.log(fg)
    ib_ref[...] = seg(5)
    gb_ref[...] = _silu(seg(6))


def _inproj_ab(x, g, w, lb, tabs, half, dqa, dkb, qdtype, tm):
    m, d = x.shape
    sw = w.shape[1] // 7
    nh = sw // (2 * dqa)
    row = lambda width: pl.BlockSpec((tm, width), lambda i: (i, 0))
    heads = pl.BlockSpec((tm, nh, 2 * dqa), lambda i: (i, 0, 0))
    outs = [(sw, qdtype), (sw, F32), (sw, F32), (sw, BF16), (sw, BF16)] + [(sw, F32)] * 5
    out_specs = [row(wd) for wd, _ in outs]
    out_shape = [jax.ShapeDtypeStruct((m, wd), dt) for wd, dt in outs]
    for i in (1, 2):
        out_specs[i] = heads
        out_shape[i] = jax.ShapeDtypeStruct((m, nh, 2 * dqa), F32)
    return pl.pallas_call(
        functools.partial(_inproj_ab_kernel, half=half, qscale=dqa ** -0.5, hscale=dkb ** -0.5),
        grid=(m // tm,),
        in_specs=[row(d), _const_spec((1, d)), _const_spec(w.shape), _const_spec((1, sw)),
                  row(LANES), row(LANES), row(LANES)],
        out_specs=out_specs,
        out_shape=out_shape,
        compiler_params=_cp(("arbitrary",)),
        name="inproj_ab",
    )(x, g.reshape(1, d), w, lb.reshape(1, sw), *tabs)


def _inproj_c_kernel(x_ref, g_ref, w_ref, ikg_ref, c128_ref, s1128_ref, s2128_ref, c64_ref, s164_ref, s264_ref,
                     q_ref, kf_ref, vf_ref, kb_ref, vb_ref, iq_ref, tail_ref, ik2_ref,
                     *, nq, nkv, niq, di, half128, half64, qscale, iwscale):
    h = _rms(x_ref[...], g_ref[...]).astype(BF16)
    c1, a1, b1 = c128_ref[...], s1128_ref[...], s2128_ref[...]
    c6, a6, b6 = c64_ref[...], s164_ref[...], s264_ref[...]
    o = 0
    q = _dot(h, w_ref[:, o:o + nq]); o += nq
    q_ref[...] = (_rope(q, c1, a1, b1, half128) * qscale).astype(q_ref.dtype)
    k = _rope(_dot(h, w_ref[:, o:o + nkv]), c1, a1, b1, half128); o += nkv
    _store_heads(kf_ref, k)
    kb_ref[...] = k.astype(BF16)
    v = _dot(h, w_ref[:, o:o + nkv]); o += nkv
    _store_heads(vf_ref, v)
    vb_ref[...] = v.astype(BF16)
    iq = _rope(_dot(h, w_ref[:, o:o + niq]), c6, a6, b6, half64); o += niq
    iq_ref[...] = iq.astype(iq_ref.dtype)
    t = _dot(h, w_ref[:, o:o + LANES])
    lane = lax.broadcasted_iota(I32, t.shape, 1)
    is_k = lane < di
    tk = jnp.where(is_k, t, 0.0)
    ms = jnp.sum(tk * tk, axis=-1, keepdims=True) * (1.0 / di)
    ikn = tk * lax.rsqrt(ms + EPS) * ikg_ref[...]
    ikr = _rope(ikn, c6, a6, b6, half64)
    ikr = jnp.where(is_k, ikr, 0.0)
    tail_ref[...] = jnp.where(is_k, ikr, t * iwscale)
    ik2_ref[...] = (ikr + pltpu.roll(ikr, di, 1)).astype(BF16)


def _inproj_c(x, g, w_pad, ikg_pad, tabs128, tabs64, half128, half64, dims, qdtype, tm):
    m, d = x.shape
    nq, nkv, niq, di, hi, dhc = dims
    row = lambda width: pl.BlockSpec((tm, width), lambda i: (i, 0))
    outs = [(nq, qdtype), (nkv, F32), (nkv, F32), (nkv, BF16), (nkv, BF16), (niq, qdtype), (LANES, F32), (LANES, BF16)]
    out_specs = [row(wd) for wd, _ in outs]
    out_shape = [jax.ShapeDtypeStruct((m, wd), dt) for wd, dt in outs]
    for i in (1, 2):
        out_specs[i] = pl.BlockSpec((tm, nkv // dhc, dhc), lambda i_: (i_, 0, 0))
        out_shape[i] = jax.ShapeDtypeStruct((m, nkv // dhc, dhc), F32)
    return pl.pallas_call(
        functools.partial(_inproj_c_kernel, nq=nq, nkv=nkv, niq=niq, di=di, half128=half128, half64=half64,
                          qscale=dhc ** -0.5, iwscale=(hi * di) ** -0.5),
        grid=(m // tm,),
        in_specs=[row(d), _const_spec((1, d)), _const_spec(w_pad.shape), _const_spec((1, LANES))]
                 + [row(LANES)] * 6,
        out_specs=out_specs,
        out_shape=out_shape,
        compiler_params=_cp(("arbitrary",)),
        name="inproj_c",
    )(x, g.reshape(1, d), w_pad, ikg_pad, *tabs128, *tabs64)


def _memproj_kernel(x_ref, g_ref, wk_ref, wv_ref, kf_ref, vf_ref, kb_ref, vb_ref):
    h = _rms(x_ref[...], g_ref[0]).astype(BF16)
    k = _dot(h, wk_ref[0])
    v = _dot(h, wv_ref[0])
    kf_ref[0] = k
    vf_ref[0] = v
    kb_ref[0] = k.astype(BF16)
    vb_ref[0] = v.astype(BF16)


def _memproj(x, g, wk, wv, tm):
    m, d = x.shape
    depth, _, n = wk.shape
    wspec = pl.BlockSpec((1, d, n), lambda l, i: (l, 0, 0))
    ospec = pl.BlockSpec((1, tm, n), lambda l, i: (l, i, 0))
    return pl.pallas_call(
        _memproj_kernel,
        grid=(depth, m // tm),
        in_specs=[pl.BlockSpec((tm, d), lambda l, i: (i, 0)), pl.BlockSpec((1, 1, d), lambda l, i: (l, 0, 0)),
                  wspec, wspec],
        out_specs=[ospec] * 4,
        out_shape=[jax.ShapeDtypeStruct((depth, m, n), dt) for dt in (F32, F32, BF16, BF16)],
        compiler_params=_cp(("arbitrary", "arbitrary")),
        name="memproj",
    )(x, g.reshape(depth, 1, d), wk, wv)


def _softmax_step(s, vb, m, l, a):
    m_new = jnp.maximum(m, jnp.max(s, axis=-1, keepdims=True))
    alpha = jnp.exp(m - m_new)
    p = jnp.exp(s - m_new)
    l = alpha * l + jnp.sum(p, axis=-1, keepdims=True)
    a = alpha * a + _dot(p.astype(BF16), vb)
    return m_new, l, a


def _diff_finish(a1, l1, a2, l2, lam, g, out_scale):
    o = a1 / l1 - lam * (a2 / l2)
    return _rms(o, g) * out_scale


def _diffattn_kernel(lam_ref, q_ref, k_ref, v_ref, g_ref, o_ref, *, tq, dq, out_scale):
    i = pl.program_id(2)
    q = q_ref[...]
    lane = lax.broadcasted_iota(I32, q.shape, 1)
    zero = jnp.zeros_like(q)
    q1 = jnp.where(lane < dq, q, zero)
    q2 = jnp.where(lane >= dq, q, zero)
    dv = v_ref.shape[1]
    causal = (lax.broadcasted_iota(I32, (tq, tq), 0) >= lax.broadcasted_iota(I32, (tq, tq), 1))
    causal_bias = jnp.where(causal, 0.0, NEG)

    def scores(j):
        kb = k_ref[pl.ds(pl.multiple_of(j * tq, tq), tq), :]
        bias = causal_bias * (j == i).astype(F32)
        return _dot_nt(q1, kb) + bias, _dot_nt(q2, kb) + bias

    def attend(j, s1, s2, state):
        vb = v_ref[pl.ds(pl.multiple_of(j * tq, tq), tq), :]
        m1, l1, a1, m2, l2, a2 = state
        return _softmax_step(s1, vb, m1, l1, a1) + _softmax_step(s2, vb, m2, l2, a2)

    def init():
        return (jnp.full((tq, 1), NEG, F32), jnp.zeros((tq, 1), F32), jnp.zeros((tq, dv), F32))

    def body(j, carry):
        s1, s2, state = carry
        n1, n2 = scores(j + 1)
        return n1, n2, attend(j, s1, s2, state)

    s1, s2, state = lax.fori_loop(0, i, body, scores(0) + (init() + init(),))
    m1, l1, a1, m2, l2, a2 = attend(i, s1, s2, state)
    o_ref[...] = _diff_finish(a1, l1, a2, l2, lam_ref[0], g_ref[...], out_scale).astype(o_ref.dtype)


def _diffattn_prompt(lam, q, kb, vb, g, b, t, nh, dq, out_scale, tq):
    m, w = q.shape
    dv = w // nh
    nq = t // tq
    smem = pl.BlockSpec(memory_space=pltpu.SMEM)
    return pl.pallas_call(
        functools.partial(_diffattn_kernel, tq=tq, dq=dq, out_scale=out_scale),
        grid=(b, nh, nq),
        in_specs=[smem,
                  pl.BlockSpec((tq, dv), lambda bb, h, i: (bb * nq + i, h)),
                  pl.BlockSpec((t, dv), lambda bb, h, i: (bb, h)),
                  pl.BlockSpec((t, dv), lambda bb, h, i: (bb, h)),
                  _const_spec((1, dv))],
        out_specs=pl.BlockSpec((tq, dv), lambda bb, h, i: (bb * nq + i, h)),
        out_shape=jax.ShapeDtypeStruct((m, w), BF16),
        compiler_params=_cp(("arbitrary",) * 3),
        name="diffattn_prompt",
    )(lam.reshape(1), q, kb, vb, g.reshape(1, dv))


def _pad_rows(x, rows):
    return jnp.concatenate([x, jnp.zeros((rows - x.shape[0], x.shape[1]), x.dtype)], axis=0)


def _head_mask_bias(rows, cols, nh, rows_per_head):
    r = lax.broadcasted_iota(I32, (rows, cols), 0)
    c = lax.broadcasted_iota(I32, (rows, cols), 1)
    own = jnp.bitwise_and(c, nh - 1) == lax.shift_right_logical(r, int(math.log2(rows_per_head)))
    return jnp.where(own, 0.0, NEG)


def _online_update(s, vlist, m_ref, l_ref, a_ref):
    m_old = m_ref[...]
    m_new = jnp.maximum(m_old, jnp.max(s, axis=-1, keepdims=True))
    alpha = jnp.exp(m_old - m_new)
    p = jnp.exp(s - m_new[:, :1]).astype(BF16)
    l_ref[...] = alpha * l_ref[...] + jnp.sum(p.astype(F32), axis=-1, keepdims=True)
    kw = s.shape[1] // len(vlist)
    pv = _dot(p[:, :kw], vlist[0])
    for r in range(1, len(vlist)):
        pv = pv + _dot(p[:, r * kw:(r + 1) * kw], vlist[r])
    a_ref[...] = alpha * a_ref[...] + pv
    m_ref[...] = m_new


def _init_softmax_state(m_ref, l_ref, a_ref):
    m_ref[...] = jnp.full(m_ref.shape, NEG, F32)
    l_ref[...] = jnp.zeros(l_ref.shape, F32)
    a_ref[...] = jnp.zeros(a_ref.shape, F32)


def _diff_decode_kernel(pt_ref, lam_ref, q_ref, kn_ref, vn_ref, g_ref, *rest, G, nh, dq, out_scale):
    kp, vp = rest[:G], rest[G:2 * G]
    o_ref, m_ref, l_ref, a_ref, hm_ref = rest[2 * G:]
    p = pl.program_id(1)
    td = q_ref.shape[0]
    dv = q_ref.shape[1] // nh
    rows = nh * 2 * td

    @pl.when(p == 0)
    def _():
        _init_softmax_state(m_ref, l_ref, a_ref)
        hm_ref[...] = _head_mask_bias(rows, hm_ref.shape[1], nh, 2 * td)

    q = q_ref[...].astype(BF16)
    lane = lax.broadcasted_iota(I32, (td, dv), 1)
    parts = []
    for h in range(nh):
        qh = q[:, h * dv:(h + 1) * dv]
        zero = jnp.zeros_like(qh)
        parts += [jnp.where(lane < dq, qh, zero), jnp.where(lane >= dq, qh, zero)]
    qall = jnp.concatenate(parts, axis=0)
    s = jnp.concatenate([_dot_nt(qall, kp[r][...].astype(BF16)) for r in range(G)], axis=1) + hm_ref[...]
    _online_update(s, [vp[r][...].astype(BF16) for r in range(G)], m_ref, l_ref, a_ref)

    @pl.when(p == pl.num_programs(1) - 1)
    def _():
        kn = _pad_rows(kn_ref[...], LANES).astype(BF16)
        vn = _pad_rows(vn_ref[...], LANES).astype(BF16)
        r = lax.broadcasted_iota(I32, (rows, LANES), 0)
        c = lax.broadcasted_iota(I32, (rows, LANES), 1)
        own = jnp.bitwise_and(c, nh - 1) == lax.shift_right_logical(r, int(math.log2(2 * td)))
        seen = lax.shift_right_logical(c, int(math.log2(nh))) <= jnp.bitwise_and(r, td - 1)
        sn = jnp.where(own, jnp.where(seen, _dot_nt(qall, kn), NEG), NEG)
        _online_update(sn, [vn], m_ref, l_ref, a_ref)
        a, l = a_ref[...], l_ref[...]
        outs = []
        for h in range(nh):
            lo = h * 2 * td
            outs.append(_diff_finish(a[lo:lo + td], l[lo:lo + td], a[lo + td:lo + 2 * td], l[lo + td:lo + 2 * td],
                                     lam_ref[0], g_ref[...], out_scale))
        o_ref[...] = jnp.concatenate(outs, axis=1)


def _diff_decode(pt, lam, q, kn, vn, g, pool_k, pool_v, base, page, bd, td, nh, dq, out_scale, G):
    m, w = q.shape
    dv = w // nh
    assert dv == LANES and nh & (nh - 1) == 0 and td & (td - 1) == 0 and td * nh <= LANES
    npg = pt.shape[1] // G
    rows = nh * 2 * td
    seq = pl.BlockSpec((td, w), lambda s, p, pt_: (s, 0))
    new = pl.BlockSpec((td * nh, dv), lambda s, p, pt_: (s, 0))

    def page_spec(r):
        return pl.BlockSpec((page * nh, dv), lambda s, p, pt_: (base + pt_[s, p * G + r], 0))

    grid_spec = pltpu.PrefetchScalarGridSpec(
        num_scalar_prefetch=1,
        grid=(bd, npg),
        in_specs=[pl.BlockSpec(memory_space=pltpu.SMEM), seq, new, new,
                  pl.BlockSpec((1, dv), lambda s, p, pt_: (0, 0))]
                 + [page_spec(r) for r in range(G)] * 2,
        out_specs=seq,
        scratch_shapes=[pltpu.VMEM((rows, LANES), F32), pltpu.VMEM((rows, LANES), F32), pltpu.VMEM((rows, dv), F32),
                        pltpu.VMEM((rows, G * page * nh), F32)],
    )
    return pl.pallas_call(
        functools.partial(_diff_decode_kernel, G=G, nh=nh, dq=dq, out_scale=out_scale),
        grid_spec=grid_spec,
        out_shape=jax.ShapeDtypeStruct((m, w), F32),
        compiler_params=_cp(("arbitrary", "arbitrary")),
        name="diffattn_decode",
    )(pt, lam.reshape(1), q, kn, vn, g.reshape(1, dv), *([pool_k] * G), *([pool_v] * G))


def _cumsum_rows(x):
    n = x.shape[0]
    row = lax.broadcasted_iota(I32, x.shape, 0)
    s = 1
    while s < n:
        x = x + jnp.where(row >= s, pltpu.roll(x, s, 0), 0.0)
        s *= 2
    return x


def _block_rows(b, blk, edge, keep):
    n = b.shape[0]
    parts = []
    for p in range(n // blk):
        lo = p * blk
        if keep(p):
            r = lo - 1 if edge == 'prev_end' else lo + blk - 1
            parts.append(jnp.broadcast_to(b[r:r + 1], (blk, b.shape[1])))
        else:
            parts.append(b[lo:lo + blk])
    return jnp.concatenate(parts, axis=0)


def _hgrn_chunk(q, k, v, lf, st, sub):
    c, dk = q.shape
    b = _cumsum_rows(lf)
    o = _dot_nt((q * jnp.exp(b)).astype(BF16), st.astype(BF16))
    row_s = lax.broadcasted_iota(I32, (sub, dk), 0)
    parts = []
    for i in range(c // sub):
        lo = i * sub
        qi, ki, vi, bi = (a[lo:lo + sub] for a in (q, k, v, b))
        acc = jnp.zeros((sub, v.shape[1]), F32)
        for s in range(sub):
            d = jnp.where(row_s >= s, bi - bi[s:s + 1], -jnp.inf)
            a = jnp.sum(jnp.exp(d) * qi * ki[s:s + 1], axis=-1, keepdims=True)
            acc = acc + a * vi[s:s + 1]
        parts.append(acc)
    o = o + (parts[0] if len(parts) == 1 else jnp.concatenate(parts, axis=0))
    if c > sub:
        row = lax.broadcasted_iota(I32, (c, dk), 0)
        tt = lax.broadcasted_iota(I32, (c, c), 0)
        ss = lax.broadcasted_iota(I32, (c, c), 1)
        att = jnp.zeros((c, c), F32)
        blk = sub
        while blk < c:
            odd = (row // blk) % 2 == 1
            rq = _block_rows(b, blk, 'prev_end', lambda p: p % 2 == 1)
            rk = _block_rows(b, blk, 'end', lambda p: p % 2 == 0)
            qs = (q * jnp.exp(jnp.where(odd, b - rq, -jnp.inf))).astype(BF16)
            ks = (k * jnp.exp(jnp.where(odd, -jnp.inf, rk - b))).astype(BF16)
            att = att + jnp.where(tt // (2 * blk) == ss // (2 * blk), _dot_nt(qs, ks), 0.0)
            blk *= 2
        o = o + _dot(att.astype(BF16), v.astype(BF16))
    bl = b[c - 1:c]
    kd = k * jnp.exp(bl - b)
    if c % LANES:
        pad = -c % LANES
        v, kd = _pad_rows(v, c + pad), _pad_rows(kd, c + pad)
    st = st * jnp.exp(bl) + _dot(v.T.astype(BF16), kd.astype(BF16))
    return o, st


def _hgrn_kernel(*refs, c, sub, has_s0):
    if has_s0:
        q_ref, k_ref, v_ref, lf_ref, gb_ref, on_ref, s0_ref, o_ref, so_ref, st_ref = refs
    else:
        q_ref, k_ref, v_ref, lf_ref, gb_ref, on_ref, o_ref, so_ref, st_ref = refs
    i = pl.program_id(2)

    @pl.when(i == 0)
    def _():
        st_ref[...] = s0_ref[0, 0].T if has_s0 else jnp.zeros(st_ref.shape, F32)

    def body(ci, carry):
        r0 = pl.multiple_of(ci * c, c)
        sl = pl.ds(r0, c)
        o, st = _hgrn_chunk(q_ref[sl, :], k_ref[sl, :], v_ref[sl, :], lf_ref[sl, :], st_ref[...], sub)
        st_ref[...] = st
        o_ref[sl, :] = (_rms(o, on_ref[...]) * gb_ref[sl, :]).astype(o_ref.dtype)
        return carry

    lax.fori_loop(0, q_ref.shape[0] // c, body, 0)

    @pl.when(i == pl.num_programs(2) - 1)
    def _():
        so_ref[0, 0] = st_ref[...].T


def _hgrn(qh, kk, ib, lf, gb, onorm, s0, s0_base, b, t, nh, tc, c, sub, odtype):
    m, w = qh.shape
    dk = w // nh
    nt = t // tc
    blk = pl.BlockSpec((tc, dk), lambda bb, h, i: (bb * nt + i, h))
    st_spec = pl.BlockSpec((1, 1, dk, dk), lambda bb, h, i: (bb, h, 0, 0))
    ins = [qh, kk, ib, lf, gb, onorm.reshape(1, dk)]
    in_specs = [blk] * 5 + [_const_spec((1, dk))]
    if s0 is not None:
        ins.append(s0)
        in_specs.append(pl.BlockSpec((1, 1, dk, dk), lambda bb, h, i: (s0_base + bb, h, 0, 0)))
    return pl.pallas_call(
        functools.partial(_hgrn_kernel, c=c, sub=sub, has_s0=s0 is not None),
        grid=(b, nh, nt),
        in_specs=in_specs,
        out_specs=[blk, st_spec],
        out_shape=[jax.ShapeDtypeStruct((m, w), odtype), jax.ShapeDtypeStruct((b, nh, dk, dk), F32)],
        scratch_shapes=[pltpu.VMEM((dk, dk), F32)],
        compiler_params=_cp(("arbitrary",) * 3),
        name="hgrn",
    )(*ins)


def _mixer_out(x, parts):
    for a_ref, w_ref in parts:
        x = x + _dot(a_ref[...].astype(BF16), w_ref[...])
    return x


def _mem_attend(q, mk_head, mv_head, nh):
    dh = q.shape[1] // nh
    outs = []
    for h in range(nh):
        s = _dot_nt(q[:, h * dh:(h + 1) * dh], mk_head(h))
        p = jnp.exp(s - jnp.max(s, axis=-1, keepdims=True))
        p = p / jnp.sum(p, axis=-1, keepdims=True)
        outs.append(_dot(p.astype(BF16), mv_head(h)))
    return jnp.concatenate(outs, axis=1).astype(BF16)


def _cross_prompt_kernel(*refs, n_parts, nh, qscale):
    x_ref = refs[0]
    parts = [(refs[1 + 2 * j], refs[2 + 2 * j]) for j in range(n_parts)]
    g_ref, wq_ref, wo_ref, mk_ref, mv_ref, o_ref = refs[1 + 2 * n_parts:]
    x1 = _mixer_out(x_ref[...], parts)
    h = _rms(x1, g_ref[...]).astype(BF16)
    q = (_dot(h, wq_ref[...]) * qscale).astype(BF16)
    dh = q.shape[1] // nh
    o = _mem_attend(q, lambda h: mk_ref[0, :, h * dh:(h + 1) * dh], lambda h: mv_ref[0, :, h * dh:(h + 1) * dh], nh)
    o_ref[...] = x1 + _dot(o, wo_ref[...])


def _cross_prompt(x, parts, g, wq, wo, mk, mv, nh, t, tm):
    m, d = x.shape
    n_mem = mk.shape[1]
    per = t // tm
    row = lambda width: pl.BlockSpec((tm, width), lambda i: (i, 0))
    mem = pl.BlockSpec((1, n_mem, d), lambda i: (i // per, 0, 0))
    ins, in_specs = [x], [row(d)]
    for a, w in parts:
        ins += [a, w]
        in_specs += [row(a.shape[1]), _const_spec(w.shape)]
    ins += [g.reshape(1, d), wq, wo, mk, mv]
    in_specs += [_const_spec((1, d)), _const_spec(wq.shape), _const_spec(wo.shape), mem, mem]
    return pl.pallas_call(
        functools.partial(_cross_prompt_kernel, n_parts=len(parts), nh=nh, qscale=(d // nh) ** -0.5),
        grid=(m // tm,),
        in_specs=in_specs,
        out_specs=row(d),
        out_shape=jax.ShapeDtypeStruct((m, d), F32),
        compiler_params=_cp(("arbitrary",)),
        name="cross_prompt",
    )(*ins)


def _cross_sample_kernel(*refs, n_parts, nh, qscale, td):
    x_ref = refs[0]
    parts = [(refs[1 + 2 * j], refs[2 + 2 * j]) for j in range(n_parts)]
    g_ref, wq_ref, wo_ref, mk_ref, mv_ref, o_ref, x1_ref, q_ref, a_ref, hm_ref = refs[1 + 2 * n_parts:]
    s = pl.program_id(0)
    dh = q_ref.shape[1] // nh

    @pl.when(s == 0)
    def _():
        x1 = _mixer_out(x_ref[...], parts)
        x1_ref[...] = x1
        q_ref[...] = _dot(_rms(x1, g_ref[...]).astype(BF16), wq_ref[...]) * qscale
        hm_ref[...] = _head_mask_bias(nh * td, hm_ref.shape[1], nh, td)

    sl = pl.ds(pl.multiple_of(s * td, td), td)
    qs = q_ref[sl, :].astype(BF16)
    qall = jnp.concatenate([qs[:, h * dh:(h + 1) * dh] for h in range(nh)], axis=0)
    sc = _dot_nt(qall, mk_ref[...].astype(BF16)) + hm_ref[...]
    p = jnp.exp(sc - jnp.max(sc, axis=-1, keepdims=True))
    p = p / jnp.sum(p, axis=-1, keepdims=True)
    o = _dot(p.astype(BF16), mv_ref[...].astype(BF16))
    a_ref[sl, :] = jnp.concatenate([o[h * td:(h + 1) * td] for h in range(nh)], axis=1)

    @pl.when(s == pl.num_programs(0) - 1)
    def _():
        o_ref[...] = x1_ref[...] + _dot(a_ref[...].astype(BF16), wo_ref[...])


def _cross_sample(x, parts, g, wq, wo, mk, mv, base, n_mem, nh, bd, td):
    m, d = x.shape
    assert nh & (nh - 1) == 0 and td & (td - 1) == 0
    mem = pl.BlockSpec((n_mem * nh, d // nh), lambda s: (base + s, 0))
    ins, in_specs = [x], [_const_spec((m, d))]
    for a, w in parts:
        ins += [a, w]
        in_specs += [_const_spec(a.shape), _const_spec(w.shape)]
    ins += [g.reshape(1, d), wq, wo, mk, mv]
    in_specs += [_const_spec((1, d)), _const_spec(wq.shape), _const_spec(wo.shape), mem, mem]
    return pl.pallas_call(
        functools.partial(_cross_sample_kernel, n_parts=len(parts), nh=nh, qscale=(d // nh) ** -0.5, td=td),
        grid=(bd,),
        in_specs=in_specs,
        out_specs=_const_spec((m, d)),
        out_shape=jax.ShapeDtypeStruct((m, d), F32),
        scratch_shapes=[pltpu.VMEM((m, d), F32)] * 3 + [pltpu.VMEM((nh * td, n_mem * nh), F32)],
        compiler_params=_cp(("arbitrary",)),
        name="cross_sample",
    )(*ins)


def _ffn_kernel(*refs, carry_mode, seq_rows, blocks_per_seq, nchunk, cw, final_norm):
    if carry_mode:
        (x_ref, g_ref, wi_ref, cwt_ref, cb_ref, wo_ref, gf_ref, o_ref, st_ref, prev_ref) = refs
    else:
        (x_ref, g_ref, wi_ref, cwt_ref, cb_ref, wo_ref, gf_ref, f1_ref, f2_ref, o_ref, gate_ref) = refs
    x = x_ref[...]
    tm = x.shape[0]
    dff = cwt_ref.shape[1]
    h = _rms(x, g_ref[...]).astype(BF16)
    row = lax.broadcasted_iota(I32, (tm, cw), 0)
    if carry_mode:
        @pl.when(pl.program_id(0) % blocks_per_seq == 0)
        def _():
            prev_ref[...] = jnp.zeros(prev_ref.shape, F32)
    else:
        rseq = row % seq_rows
    acc = jnp.zeros(x.shape, F32)
    for ci in range(nchunk):
        c0 = ci * cw
        cs = slice(c0, c0 + cw)
        gt = _dot(h, wi_ref[:, cs])
        u = _dot(h, wi_ref[:, dff + c0:dff + c0 + cw])
        if carry_mode:
            p0, p1 = prev_ref[0:1, cs], prev_ref[1:2, cs]
            g1 = jnp.where(row == 0, p1, pltpu.roll(gt, 1, 0))
            g2 = jnp.where(row == 0, p0, jnp.where(row == 1, p1, pltpu.roll(gt, 2, 0)))
            last = gt[tm - 2:tm]
            prev_ref[:, cs] = last
            st_ref[0, :, cs] = last
        else:
            g1 = jnp.where(rseq >= 1, pltpu.roll(gt, 1, 0), f1_ref[:, cs])
            g2 = jnp.where(rseq >= 2, pltpu.roll(gt, 2, 0), f2_ref[:, cs])
            gate_ref[:, cs] = gt
        gc = cb_ref[:, cs] + g2 * cwt_ref[0:1, cs] + g1 * cwt_ref[1:2, cs] + gt * cwt_ref[2:3, cs]
        acc = acc + _dot((_silu(gc) * u).astype(BF16), wo_ref[cs, :])
    y = x + acc
    if final_norm:
        y = _rms(y, gf_ref[...])
    o_ref[...] = y


def _ffn(x, g, wi, cwt, cb, wo, gfinal, final_norm, seq_rows, tm, fills=None):
    m, d = x.shape
    dff = wo.shape[0]
    cw = 256
    carry_mode = fills is None
    nseq = m // seq_rows
    row = lambda width: pl.BlockSpec((tm, width), lambda i: (i, 0))
    ins = [x, g.reshape(1, d), wi, cwt, cb.reshape(1, dff), wo, gfinal.reshape(1, d)]
    in_specs = [row(d), _const_spec((1, d)), _const_spec(wi.shape), _const_spec(cwt.shape),
                _const_spec((1, dff)), _const_spec(wo.shape), _const_spec((1, d))]
    if carry_mode:
        per = seq_rows // tm
        out_specs = [row(d), pl.BlockSpec((1, 2, dff), lambda i: (i // per, 0, 0))]
        out_shape = [jax.ShapeDtypeStruct((m, d), F32), jax.ShapeDtypeStruct((nseq, 2, dff), F32)]
        scratch = [pltpu.VMEM((2, dff), F32)]
    else:
        per = 1
        ins += list(fills)
        in_specs += [row(dff), row(dff)]
        out_specs = [row(d), row(dff)]
        out_shape = [jax.ShapeDtypeStruct((m, d), F32), jax.ShapeDtypeStruct((m, dff), F32)]
        scratch = []
    return pl.pallas_call(
        functools.partial(_ffn_kernel, carry_mode=carry_mode, seq_rows=seq_rows, blocks_per_seq=per,
                          nchunk=dff // cw, cw=cw, final_norm=final_norm),
        grid=(m // tm,),
        in_specs=in_specs,
        out_specs=out_specs,
        out_shape=out_shape,
        scratch_shapes=scratch,
        compiler_params=_cp(("arbitrary",)),
        name="ffn",
    )(*ins)


LOWEST_KEY = INT_MIN + 0x00800000


def _key_to_f32(key):
    return lax.bitcast_convert_type(jnp.where(key < 0, key ^ 0x7FFFFFFF, key), F32)


def _fold_lanes(c):
    parts = [c[:, j * LANES:(j + 1) * LANES] for j in range(c.shape[1] // LANES)]
    while len(parts) > 1:
        parts = [parts[j] + parts[j + 1] for j in range(0, len(parts) - 1, 2)] + (parts[-1:] if len(parts) % 2 else [])
    return parts[0]


def _select_topk(sc_ref, n_tiles, topk, idx_bits):
    _, rows, tw = sc_ref.shape
    lane = lax.broadcasted_iota(I32, (rows, tw), 1)

    def count(ind):
        def body(j, acc):
            return acc + _fold_lanes(ind(sc_ref[j], j))
        acc = lax.fori_loop(0, n_tiles, body, jnp.zeros((rows, LANES), F32))
        return jnp.sum(acc, axis=-1, keepdims=True)

    def thr_step(it, key):
        trial = key ^ jnp.left_shift(jnp.int32(1), 31 - it)
        tf = _key_to_f32(trial)
        cnt = count(lambda kt, j: jnp.where(kt >= tf, 1.0, 0.0))
        return jnp.where(cnt >= topk, trial, key)

    key = lax.fori_loop(0, 32, thr_step, jnp.full((rows, 1), INT_MIN, I32))
    thr = _key_to_f32(jnp.maximum(key, LOWEST_KEY))
    need = topk - count(lambda kt, j: jnp.where(kt > thr, 1.0, 0.0))
    ties = count(lambda kt, j: jnp.where(kt == thr, 1.0, 0.0))
    excess = jnp.max(ties - need) > 0.0

    def cut_step(it, cut):
        trial = cut | jnp.left_shift(jnp.int32(1), idx_bits - 1 - it)
        cnt = count(lambda kt, j: jnp.where(kt == thr, jnp.where(lane + j * tw < trial, 1.0, 0.0), 0.0))
        return jnp.where(cnt <= need, trial, cut)

    cut0 = jnp.full((rows, 1), jnp.where(excess, 0, 1 << idx_bits), I32)
    cut = lax.fori_loop(0, jnp.where(excess, idx_bits, 0), cut_step, cut0)
    return thr, cut


def _selection_bias(kt, j, tw, thr, cut):
    lane = lax.broadcasted_iota(I32, kt.shape, 1)
    return jnp.where(kt > thr, 0.0, jnp.where(kt == thr, jnp.where(lane + j * tw < cut, 0.0, NEG), NEG))


def _index_scores(iq, iw, ik2_tile, nhi, di):
    lane = lax.broadcasted_iota(I32, (iq.shape[0], LANES), 1)
    acc = None
    for pr in range(nhi * di // LANES):
        pair = iq[:, pr * LANES:(pr + 1) * LANES]
        zero = jnp.zeros_like(pair)
        for half_i, qm in enumerate((jnp.where(lane < di, pair, zero), jnp.where(lane >= di, pair, zero))):
            hd = 2 * pr + half_i
            term = jnp.maximum(_dot_nt(qm, ik2_tile), 0.0) * iw[:, hd:hd + 1]
            acc = term if acc is None else acc + term
    return acc


def _dsa_index_kernel(iq_ref, tail_ref, ik2_ref, bias_ref, keys_ref, *, tq, tw, topk, nhi, di, idx_bits):
    i = pl.program_id(1)
    nt_all = keys_ref.shape[0]
    n_tiles = (i * tq + tq + tw - 1) // tw
    iq = iq_ref[...]
    iw = tail_ref[:, di:di + nhi]
    qpos = i * tq + lax.broadcasted_iota(I32, (tq, tw), 0)
    lane = lax.broadcasted_iota(I32, (tq, tw), 1)

    def score_tile(j, c):
        r0 = pl.multiple_of(j * tw, tw)
        sc = _index_scores(iq, iw, ik2_ref[pl.ds(r0, tw), :], nhi, di)
        keys_ref[j] = jnp.where(lane + j * tw <= qpos, sc, -jnp.inf)
        return c

    lax.fori_loop(0, n_tiles, score_tile, 0)
    thr, cut = _select_topk(keys_ref, n_tiles, topk, idx_bits)

    def write_tile(j, c):
        bias_ref[0, j] = _selection_bias(keys_ref[j], j, tw, thr, cut).astype(bias_ref.dtype)
        return c

    lax.fori_loop(0, n_tiles, write_tile, 0)

    def fill_tile(j, c):
        bias_ref[0, j] = jnp.full((tq, tw), NEG, bias_ref.dtype)
        return c

    lax.fori_loop(n_tiles, nt_all, fill_tile, 0)


def _dsa_index_prompt(iq, tail, ik2, b, t, topk, nhi, di, tq, tw):
    m = iq.shape[0]
    nq = t // tq
    nt = t // tw
    return pl.pallas_call(
        functools.partial(_dsa_index_kernel, tq=tq, tw=tw, topk=topk, nhi=nhi, di=di,
                          idx_bits=max(1, t.bit_length())),
        grid=(b, nq),
        in_specs=[pl.BlockSpec((tq, iq.shape[1]), lambda bb, i: (bb * nq + i, 0)),
                  pl.BlockSpec((tq, LANES), lambda bb, i: (bb * nq + i, 0)),
                  pl.BlockSpec((t, LANES), lambda bb, i: (bb, 0))],
        out_specs=pl.BlockSpec((1, nt, tq, tw), lambda bb, i: (bb * nq + i, 0, 0, 0)),
        out_shape=jax.ShapeDtypeStruct((m // tq, nt, tq, tw), BF16),
        scratch_shapes=[pltpu.VMEM((nt, tq, tw), F32)],
        compiler_params=_cp(("arbitrary", "arbitrary")),
        name="dsa_index",
    )(iq, tail, ik2)


def _gqa_update(g, s, vlist, m_ref, l_ref, a_ref):
    m_old = m_ref[g]
    m_new = jnp.maximum(m_old, jnp.max(s, axis=-1, keepdims=True))
    alpha = jnp.exp(m_old - m_new)
    p = jnp.exp(s - m_new[:, :1])
    l_ref[g] = alpha * l_ref[g] + jnp.sum(p, axis=-1, keepdims=True)
    pb = p.astype(BF16)
    kw = s.shape[1] // len(vlist)
    pv = _dot(pb[:, :kw], vlist[0])
    for r in range(1, len(vlist)):
        pv = pv + _dot(pb[:, r * kw:(r + 1) * kw], vlist[r])
    a_ref[g] = alpha * a_ref[g] + pv
    m_ref[g] = m_new


def _stack_heads(q, g, per, dh):
    return jnp.concatenate([q[:, (g * per + hl) * dh:(g * per + hl + 1) * dh] for hl in range(per)], axis=0)


def _unstack_heads(a_ref, l_ref, nkv, per, rows):
    outs = []
    for g in range(nkv):
        o = a_ref[g] / l_ref[g][:, :1]
        outs += [o[hl * rows:(hl + 1) * rows] for hl in range(per)]
    return jnp.concatenate(outs, axis=1)


def _dsa_attn_kernel(q_ref, k_ref, v_ref, bias_ref, o_ref, m_ref, l_ref, a_ref, *, tq, tw, nkv, per, dh):
    i = pl.program_id(1)
    n_tiles = (i * tq + tq + tw - 1) // tw
    m_ref[...] = jnp.full(m_ref.shape, NEG, F32)
    l_ref[...] = jnp.zeros(l_ref.shape, F32)
    a_ref[...] = jnp.zeros(a_ref.shape, F32)
    q = q_ref[...]
    qg = [_stack_heads(q, g, per, dh) for g in range(nkv)]

    def body(j, c):
        r0 = pl.multiple_of(j * tw, tw)
        bias = bias_ref[0, j].astype(F32)
        bias = jnp.concatenate([bias] * per, axis=0)
        for g in range(nkv):
            kb = k_ref[pl.ds(r0, tw), g * dh:(g + 1) * dh]
            vb = v_ref[pl.ds(r0, tw), g * dh:(g + 1) * dh]
            _gqa_update(g, _dot_nt(qg[g], kb) + bias, [vb], m_ref, l_ref, a_ref)
        return c

    lax.fori_loop(0, n_tiles, body, 0)
    o_ref[...] = _unstack_heads(a_ref, l_ref, nkv, per, tq).astype(o_ref.dtype)


def _dsa_attn_prompt(q, kb, vb, bias, b, t, nkv, tq, tw):
    m, w = q.shape
    dh = kb.shape[1] // nkv
    per = w // dh // nkv
    nq = t // tq
    nt = t // tw
    return pl.pallas_call(
        functools.partial(_dsa_attn_kernel, tq=tq, tw=tw, nkv=nkv, per=per, dh=dh),
        grid=(b, nq),
        in_specs=[pl.BlockSpec((tq, w), lambda bb, i: (bb * nq + i, 0)),
                  pl.BlockSpec((t, nkv * dh), lambda bb, i: (bb, 0)),
                  pl.BlockSpec((t, nkv * dh), lambda bb, i: (bb, 0)),
                  pl.BlockSpec((1, nt, tq, tw), lambda bb, i: (bb * nq + i, 0, 0, 0))],
        out_specs=pl.BlockSpec((tq, w), lambda bb, i: (bb * nq + i, 0)),
        out_shape=jax.ShapeDtypeStruct((m, w), BF16),
        scratch_shapes=[pltpu.VMEM((nkv, per * tq, LANES), F32), pltpu.VMEM((nkv, per * tq, LANES), F32),
                        pltpu.VMEM((nkv, per * tq, dh), F32)],
        compiler_params=_cp(("arbitrary", "arbitrary")),
        name="dsa_attn",
    )(q, kb, vb, bias)


def _dsa_dec_scores_kernel(pt_ref, iq_ref, iw_ref, ikn_ref, *rest, G, td, nhi, page, wnew):
    ikp = rest[:G]
    past_ref, new_ref = rest[G:]
    iq = iq_ref[0].astype(BF16)
    iw = iw_ref[0]

    def scores(s):
        s = jnp.maximum(s, 0.0) * iw
        out = s[0:td]
        for h in range(1, nhi):
            out = out + s[h * td:(h + 1) * td]
        return out

    past_ref[0] = jnp.concatenate([scores(_dot(iq, ikp[r][0].astype(BF16))) for r in range(G)], axis=1)

    @pl.when(pl.program_id(1) == pl.num_programs(1) - 1)
    def _():
        sn = scores(_dot_nt(iq, _pad_rows(ikn_ref[...], wnew).astype(BF16)))
        row = lax.broadcasted_iota(I32, sn.shape, 0)
        col = lax.broadcasted_iota(I32, sn.shape, 1)
        new_ref[0] = jnp.where(col <= row, sn, -jnp.inf)


def _dsa_dec_scores(pt, iq_hm, iw_hm, ik_new, pool_ikt, base, bd, td, nhi, G, wnew):
    di = pool_ikt.shape[1]
    page = pool_ikt.shape[2]
    npg = pt.shape[1] // G
    seq3 = lambda shape: pl.BlockSpec((1,) + shape, lambda s, p, pt_: (s, 0, 0))
    grid_spec = pltpu.PrefetchScalarGridSpec(
        num_scalar_prefetch=1,
        grid=(bd, npg),
        in_specs=[seq3((nhi * td, di)), seq3((nhi * td, 1)), pl.BlockSpec((td, di), lambda s, p, pt_: (s, 0))]
                 + [pl.BlockSpec((1, di, page), functools.partial(lambda s, p, pt_, r: (base + pt_[s, p * G + r], 0, 0), r=r))
                    for r in range(G)],
        out_specs=[pl.BlockSpec((1, td, G * page), lambda s, p, pt_: (s, 0, p)), seq3((td, wnew))],
    )
    return pl.pallas_call(
        functools.partial(_dsa_dec_scores_kernel, G=G, td=td, nhi=nhi, page=page, wnew=wnew),
        grid_spec=grid_spec,
        out_shape=[jax.ShapeDtypeStruct((bd, td, pt.shape[1] * page), F32), jax.ShapeDtypeStruct((bd, td, wnew), F32)],
        compiler_params=_cp(("arbitrary", "arbitrary")),
        name="dsa_dec_scores",
    )(pt, iq_hm, iw_hm, ik_new, *([pool_ikt] * G))


def _dsa_dec_select_kernel(sc_ref, bias_ref, *, topk, idx_bits):
    nt, rows, tw = sc_ref.shape
    thr, cut = _select_topk(sc_ref, nt, topk, idx_bits)
    for j in range(nt):
        bias_ref[j] = _selection_bias(sc_ref[j], j, tw, thr, cut)


def _dsa_dec_select(sc_tiles, topk, rows):
    nt, m, tw = sc_tiles.shape
    spec = pl.BlockSpec((nt, rows, tw), lambda i: (0, i, 0))
    return pl.pallas_call(
        functools.partial(_dsa_dec_select_kernel, topk=topk, idx_bits=max(1, (nt * tw).bit_length())),
        grid=(m // rows,),
        in_specs=[spec],
        out_specs=spec,
        out_shape=jax.ShapeDtypeStruct((nt, m, tw), F32),
        compiler_params=_cp(("arbitrary",)),
        name="dsa_dec_select",
    )(sc_tiles)


def _dsa_dec_attn_kernel(pt_ref, q_ref, kn_ref, vn_ref, bias_ref, bnew_ref, *rest, G, td, nh, nkv):
    kp, vp = rest[:G], rest[G:2 * G]
    o_ref, m_ref, l_ref, a_ref, hm_ref = rest[2 * G:]
    p = pl.program_id(1)
    dh = q_ref.shape[1] // nh
    rows = nh * td

    @pl.when(p == 0)
    def _():
        _init_softmax_state(m_ref, l_ref, a_ref)
        hm_ref[...] = _head_mask_bias(rows, hm_ref.shape[1], nkv, rows // nkv)

    q = q_ref[...].astype(BF16)
    qall = jnp.concatenate([q[:, h * dh:(h + 1) * dh] for h in range(nh)], axis=0)
    bias = jnp.concatenate([bias_ref[j] for j in range(bias_ref.shape[0])], axis=1)
    bias = jnp.concatenate([bias] * nh, axis=0) + hm_ref[...]
    s = jnp.concatenate([_dot_nt(qall, kp[r][...].astype(BF16)) for r in range(G)], axis=1) + bias
    _online_update(s, [vp[r][...].astype(BF16) for r in range(G)], m_ref, l_ref, a_ref)

    @pl.when(p == pl.num_programs(1) - 1)
    def _():
        kn = _pad_rows(kn_ref[...], LANES).astype(BF16)
        vn = _pad_rows(vn_ref[...], LANES).astype(BF16)
        bias_n = jnp.concatenate([bnew_ref[0, :, 0:LANES]] * nh, axis=0) + hm_ref[:, 0:LANES]
        _online_update(_dot_nt(qall, kn) + bias_n, [vn], m_ref, l_ref, a_ref)
        o = a_ref[...] / l_ref[...][:, :1]
        o_ref[...] = jnp.concatenate([o[h * td:(h + 1) * td] for h in range(nh)], axis=1)


def _dsa_dec_attn(pt, q, kn, vn, bias2, pool_k, pool_v, base, page, bd, td, nkv, G):
    m, w = q.shape
    dh = pool_k.shape[1]
    nh = w // dh
    assert dh == LANES and nkv & (nkv - 1) == 0 and td * nkv <= LANES
    npg = pt.shape[1] // G
    nt, _, tw2 = bias2.shape
    tiles_per_step = G * page * nkv // tw2
    rows = nh * td
    seq = pl.BlockSpec((td, w), lambda s, p, pt_: (s, 0))
    new = pl.BlockSpec((td * nkv, dh), lambda s, p, pt_: (s, 0))

    def page_spec(r):
        return pl.BlockSpec((page * nkv, dh), lambda s, p, pt_: (base + pt_[s, p * G + r], 0))

    grid_spec = pltpu.PrefetchScalarGridSpec(
        num_scalar_prefetch=1,
        grid=(bd, npg),
        in_specs=[seq, new, new,
                  pl.BlockSpec((tiles_per_step, td, tw2), lambda s, p, pt_: (p, s, 0)),
                  pl.BlockSpec((1, td, tw2), lambda s, p, pt_: (nt - 1, s, 0))]
                 + [page_spec(r) for r in range(G)] * 2,
        out_specs=seq,
        scratch_shapes=[pltpu.VMEM((rows, LANES), F32), pltpu.VMEM((rows, LANES), F32), pltpu.VMEM((rows, dh), F32),
                        pltpu.VMEM((rows, G * page * nkv), F32)],
    )
    return pl.pallas_call(
        functools.partial(_dsa_dec_attn_kernel, G=G, td=td, nh=nh, nkv=nkv),
        grid_spec=grid_spec,
        out_shape=jax.ShapeDtypeStruct((m, w), F32),
        compiler_params=_cp(("arbitrary", "arbitrary")),
        name="dsa_dec_attn",
    )(pt, q, kn, vn, bias2, bias2, *([pool_k] * G), *([pool_v] * G))


def kernel(x_prompt, x_sample, cache_diff_k, cache_diff_v, state_hgrn, cache_dsa_k, cache_dsa_v, cache_dsa_ik,
           cache_mem_k, cache_mem_v, state_ffn_conv, page_table, mem_prompt, norm_mix, w_in_ab, diff_lq1, diff_lk1,
           diff_lq2, diff_lk2, diff_subln, hgrn_lb_logits, hgrn_onorm, w_out_ab, w_in_c, idx_k_norm, w_out_c,
           norm_x, norm_mem, w_xq, w_xk, w_xv, w_xo, norm_ffn, w_ffn_in, conv_ffn_w, conv_ffn_b, w_ffn_out,
           norm_final):
    b, t, d = x_prompt.shape
    bd, td, _ = x_sample.shape
    depth = norm_mix.shape[0]
    n_ab = w_in_ab.shape[0]
    ha, dva = cache_diff_v.shape[3], cache_diff_v.shape[4]
    dqa = cache_diff_k.shape[4] // 2
    hb, dkb = state_hgrn.shape[2], state_hgrn.shape[3]
    kvc, dhc = cache_dsa_k.shape[3], cache_dsa_k.shape[4]
    hc = d // dhc
    di = cache_dsa_ik.shape[3]
    hi = w_in_c.shape[2] - (hc + 2 * kvc) * dhc - di
    hi = hi // (di + 1)
    hx = cache_mem_k.shape[3]
    n_mem = mem_prompt.shape[1]
    dff = w_ffn_out.shape[1]
    n_pool, page = cache_diff_k.shape[1], cache_diff_k.shape[2]
    past = page_table.shape[1] * page
    mp, ms = b * t, bd * td
    G = math.gcd(page_table.shape[1], 16)

    pos_p = jnp.tile(jnp.arange(t), b)
    pos_s = past + jnp.tile(jnp.arange(td), bd)
    tab64_p, half64 = _rope_tables(pos_p, dqa)
    tab64_s, _ = _rope_tables(pos_s, dqa)
    tab128_p, half128 = _rope_tables(pos_p, dhc)
    tab128_s, _ = _rope_tables(pos_s, dhc)

    hgrn_lb = jnp.cumsum(jax.nn.softmax(hgrn_lb_logits.astype(F32), axis=0), axis=0)[:n_ab]
    xp = x_prompt.reshape(mp, d)
    xs = x_sample.reshape(ms, d)
    gfin = norm_final

    mkf, mvf, mkb, mvb = _memproj(mem_prompt.reshape(b * n_mem, d), norm_mem, w_xk.astype(BF16), w_xv.astype(BF16), 256)

    new_p = {k: [] for k in ('diff_k', 'diff_v', 'hgrn', 'dsa_k', 'dsa_v', 'dsa_ik', 'conv')}
    new_s = {k: [] for k in new_p}

    for l in range(depth):
        j = l // 2
        wq_b, wo_b = w_xq[l].astype(BF16), w_xo[l].astype(BF16)
        if l % 2 == 0:
            w_in = w_in_ab[j].astype(BF16)
            w_out = w_out_ab[j].astype(BF16)
            wa, wb = w_out[:ha * dva], w_out[ha * dva:]
            lam_init = 0.8 - 0.6 * math.exp(-0.3 * l)
            lam = (jnp.exp(jnp.sum(diff_lq1[j].astype(F32) * diff_lk1[j].astype(F32)))
                   - jnp.exp(jnp.sum(diff_lq2[j].astype(F32) * diff_lk2[j].astype(F32))) + lam_init)
            q, kf, vf, kb, vb, qh, kk, lf, ib, gb = _inproj_ab(
                xp, norm_mix[l], w_in, hgrn_lb[j], tab64_p, half64, dqa, dkb, BF16, 512)
            o_a = _diffattn_prompt(lam, q, kb, vb, diff_subln[j], b, t, ha, dqa, 1.0 - lam_init, 256)
            o_b, s_new = _hgrn(qh, kk, ib, lf, gb, hgrn_onorm[j], None, 0, b, t, hb, 512, 128, 8, BF16)
            new_p['diff_k'].append(kf.reshape(1, b, t, ha, 2 * dqa))
            new_p['diff_v'].append(vf.reshape(1, b, t, ha, dva))
            new_p['hgrn'].append(s_new[None])
            parts_p = [(o_a, wa), (o_b, wb)]
            q, kf, vf, kb, vb, qh, kk, lf, ib, gb = _inproj_ab(
                xs, norm_mix[l], w_in, hgrn_lb[j], tab64_s, half64, dqa, dkb, F32, ms)
            pool_k = cache_diff_k.reshape(-1, 2 * dqa)
            pool_v = cache_diff_v.reshape(-1, dva)
            o_a = _diff_decode(page_table, lam, q, kf.reshape(-1, 2 * dqa), vf.reshape(-1, dva), diff_subln[j],
                               pool_k, pool_v, j * n_pool, page, bd, td, ha, dqa, 1.0 - lam_init, G)
            c_s = math.gcd(td, 64)
            o_b, s_new = _hgrn(qh, kk, ib, lf, gb, hgrn_onorm[j], state_hgrn.reshape(-1, hb, dkb, dkb), j * bd,
                               bd, td, hb, td, c_s, min(8, c_s), F32)
            new_s['diff_k'].append(kf.reshape(1, bd, td, ha, 2 * dqa))
            new_s['diff_v'].append(vf.reshape(1, bd, td, ha, dva))
            new_s['hgrn'].append(s_new[None])
            parts_s = [(o_a, wa), (o_b, wb)]
        else:
            w_in = w_in_c[j].astype(BF16)
            cw = w_in.shape[1]
            main = (hc + 2 * kvc) * dhc + hi * di
            w_pad = jnp.concatenate([w_in, jnp.zeros((d, main + LANES - cw), BF16)], axis=1)
            ikg_pad = jnp.concatenate([idx_k_norm[j].astype(F32), jnp.zeros((LANES - di,), F32)]).reshape(1, LANES)
            dims = (hc * dhc, kvc * dhc, hi * di, di, hi, dhc)
            w_out = w_out_c[j].astype(BF16)
            q, kf, vf, kb, vb, iq, tail, ik2 = _inproj_c(
                xp, norm_mix[l], w_pad, ikg_pad, tab128_p, tab64_p, half128, half64, dims, BF16, 512)
            topk = min(DSA_TOPK_MAX, t // 4)
            bias = _dsa_index_prompt(iq, tail, ik2, b, t, topk, hi, di, 128, 512)
            o_c = _dsa_attn_prompt(q, kb, vb, bias, b, t, kvc, 128, 512)
            new_p['dsa_k'].append(kf.reshape(1, b, t, kvc, dhc))
            new_p['dsa_v'].append(vf.reshape(1, b, t, kvc, dhc))
            new_p['dsa_ik'].append(tail[:, :di].reshape(1, b, t, di))
            parts_p = [(o_c, w_out)]
            q, kf, vf, kb, vb, iq, tail, ik2 = _inproj_c(
                xs, norm_mix[l], w_pad, ikg_pad, tab128_s, tab64_s, half128, half64, dims, F32, ms)
            tw = 512
            iq_hm = iq.reshape(bd, td, hi, di).transpose(0, 2, 1, 3).reshape(bd, hi * td, di)
            iw_hm = tail[:, di:di + hi].reshape(bd, td, hi).transpose(0, 2, 1).reshape(bd, hi * td, 1)
            ik_new = tail[:, :di]
            pool_ikt = jnp.swapaxes(cache_dsa_ik.reshape(-1, page, di), 1, 2)
            sc_past, sc_new = _dsa_dec_scores(page_table, iq_hm, iw_hm, ik_new, pool_ikt, j * n_pool, bd, td, hi, G, tw)
            sc = jnp.concatenate([sc_past, sc_new], axis=2).reshape(ms, -1)
            nt = sc.shape[1] // tw
            sc_tiles = sc.reshape(ms, nt, tw).transpose(1, 0, 2)
            topk = min(DSA_TOPK_MAX, (past + td) // 4)
            bias_tiles = _dsa_dec_select(sc_tiles, topk, 128 if ms % 128 == 0 else ms)
            bias2 = jnp.repeat(bias_tiles, kvc, axis=2)
            o_c = _dsa_dec_attn(page_table, q, kf.reshape(-1, dhc), vf.reshape(-1, dhc), bias2,
                                cache_dsa_k.reshape(-1, dhc), cache_dsa_v.reshape(-1, dhc), j * n_pool, page, bd, td,
                                kvc, G)
            new_s['dsa_k'].append(kf.reshape(1, bd, td, kvc, dhc))
            new_s['dsa_v'].append(vf.reshape(1, bd, td, kvc, dhc))
            new_s['dsa_ik'].append(ik_new.reshape(1, bd, td, di))
            parts_s = [(o_c, w_out)]

        last = l == depth - 1
        wi_b, wo2_b = w_ffn_in[l].astype(BF16), w_ffn_out[l].astype(BF16)
        xp = _cross_prompt(xp, parts_p, norm_x[l], wq_b, wo_b, mkb[l].reshape(b, n_mem, d), mvb[l].reshape(b, n_mem, d),
                           hx, t, 512)
        xp, conv_p = _ffn(xp, norm_ffn[l], wi_b, conv_ffn_w[l], conv_ffn_b[l], wo2_b, gfin, last, t, 512)
        new_p['conv'].append(conv_p[None])
        xs = _cross_sample(xs, parts_s, norm_x[l], wq_b, wo_b, cache_mem_k.reshape(-1, d // hx),
                           cache_mem_v.reshape(-1, d // hx), l * bd, n_mem, hx, bd, td)
        st = state_ffn_conv[l]
        zeros = jnp.zeros((bd, td - 1, dff), F32)
        fill1 = jnp.concatenate([st[:, 1:2], zeros], axis=1).reshape(ms, dff)
        fill2 = jnp.concatenate([st, zeros[:, 1:]], axis=1).reshape(ms, dff)
        xs, gate_s = _ffn(xs, norm_ffn[l], wi_b, conv_ffn_w[l], conv_ffn_b[l], wo2_b, gfin, last, td, ms,
                          fills=(fill1, fill2))
        new_s['conv'].append(gate_s.reshape(bd, td, dff)[None, :, td - 2:])

    join = lambda v: v[0] if len(v) == 1 else jnp.concatenate(v, axis=0)
    sp = {k: join(v) for k, v in new_p.items()}
    ss = {k: join(v) for k, v in new_s.items()}
    mk_p = mkf.reshape(depth, b, n_mem, hx, d // hx)
    mv_p = mvf.reshape(depth, b, n_mem, hx, d // hx)
    return (xp.reshape(b, t, d), xs.reshape(bd, td, d),
            sp['diff_k'], sp['diff_v'], sp['hgrn'], sp['dsa_k'], sp['dsa_v'], sp['dsa_ik'],
            mk_p, mv_p, sp['conv'],
            ss['diff_k'], ss['diff_v'], ss['hgrn'], ss['dsa_k'], ss['dsa_v'], ss['dsa_ik'], ss['conv'])
```
